```python
import jax, jax.numpy as jnp
from jax import lax
import numpy as np

D_MODEL = 1024
BATCH = 32
SEQ = 256
DEPTH = 4
DEC_BATCH = 4
DEC_SEQ = 2048
PAST_LEN = 256

GRID_W = 64
D_MIX = D_MODEL
WA = D_MIX // 2
HEAD_A = 64
HA = WA // HEAD_A
WB = D_MIX - WA
HB = 8
BW = WB // HB
R_W = 64
R_A = 64
R_G = 128
D_FF = 2816
CONV_B = 4
CONV_B_LEFT = 2
CONV_F = 3
CONV_F_LEFT = 1
LRU_C = 8.0
N_DIR = 2
EPS = 1e-6
LNX_EPS = 64e-5
KK_EPS = 1e-12
DECAY_SCALE = 0.6065306597126334
P_IN = 3 * WA + R_W + R_A + R_G + 2 * WB
SPLITS = (WA, 2 * WA, 3 * WA, 3 * WA + R_W, 3 * WA + R_W + R_A, 3 * WA + R_W + R_A + R_G,
          3 * WA + R_W + R_A + R_G + WB)

kernel_name = 'hybrid_rwkv7_rglru_prefix_diffusion_step'


def rmsnorm(x, g):
    xf = x.astype(jnp.float32)
    y = xf * lax.rsqrt(jnp.mean(xf * xf, axis=-1, keepdims=True) + EPS)
    return (y * g.astype(jnp.float32)).astype(x.dtype)


def dwconv(x, w, b, left, grid):
    bsz, L, C = x.shape
    K = w.shape[0]
    if grid:
        rows = L // GRID_W
        x = x.reshape(bsz * rows, GRID_W, C)
    n = x.shape[1]
    xp = jnp.pad(x, ((0, 0), (left, K - 1 - left), (0, 0)))
    y = b + xp[:, 0:n, :] * w[0]
    for j in range(1, K):
        y = y + xp[:, j:j + n, :] * w[j]
    return y.reshape(bsz, L, C)


def _heads(t):
    return t.reshape(t.shape[:-1] + (HA, HEAD_A))


def dir_time_major(x2):
    x2 = jnp.stack([x2[0], jnp.flip(x2[1], axis=1)])
    return jnp.moveaxis(x2, 2, 0)


def merge_dirs(y):
    y = jnp.moveaxis(y, 0, 2)
    return y[0] + jnp.flip(y[1], axis=1)


def _wkv7_step(S, inp):
    r_t, w_t, kk_t, b_t, k_t, v_t = inp
    sa = jnp.einsum('dbhij,dbhj->dbhi', S, kk_t)
    S = S * w_t[..., None, :] - sa[..., :, None] * b_t[..., None, :] + v_t[..., :, None] * k_t[..., None, :]
    y = jnp.einsum('dbhij,dbhj->dbhi', S, r_t)
    return S, y


def rwkv7_mix(r, k, v, xw, xa, xg, s0, w0, w_up, a0, a_up, g_up, k_k, k_a, r_k, lnx_g, lnx_b):
    bsz, T, _ = r.shape
    dt = r.dtype
    f32 = jnp.float32
    r, k, v = r.astype(f32), k.astype(f32), v.astype(f32)
    logit_w = w0[:, None, None, :] + jnp.einsum('btr,drc->dbtc', jnp.tanh(xw), w_up)
    decay = jnp.exp(-DECAY_SCALE * jax.nn.sigmoid(logit_w.astype(f32)))
    a = jax.nn.sigmoid((a0[:, None, None, :] + jnp.einsum('btr,drc->dbtc', xa, a_up)).astype(f32))
    g = jax.nn.sigmoid(xg) @ g_up
    kk = _heads(k * k_k)
    kk = kk * lax.rsqrt(jnp.sum(kk * kk, axis=-1, keepdims=True) + KK_EPS)
    k_dir = k[None] * (1.0 + (a - 1.0) * k_a)
    rh, vh, kdh = _heads(r), _heads(v), _heads(k_dir)
    both = lambda t: jnp.broadcast_to(t, (N_DIR,) + t.shape)
    xs = (dir_time_major(both(rh)), dir_time_major(_heads(decay)), dir_time_major(both(kk)),
          dir_time_major(both(kk) * _heads(a)), dir_time_major(kdh), dir_time_major(both(vh)))
    s_fin, y = lax.scan(_wkv7_step, s0.astype(f32), xs)
    o = merge_dirs(y)
    mu = jnp.mean(o, axis=-1, keepdims=True)
    var = jnp.mean(jnp.square(o - mu), axis=-1, keepdims=True)
    o = ((o - mu) * lax.rsqrt(var + LNX_EPS)).reshape(bsz, T, WA) * lnx_g + lnx_b
    bonus = jnp.sum(rh[None] * kdh * r_k, axis=(0, -1))[..., None] * vh
    out = (o + bonus.reshape(bsz, T, WA)) * g
    return out.astype(dt), s_fin


def _lin_combine(e1, e2):
    a1, b1 = e1
    a2, b2 = e2
    return a1 * a2, a2 * b1 + b2


def rglru_mix(xb, gb, h0, conv_w, conv_b, ga_w, ga_b, gx_w, gx_b, lam, grid):
    bsz, T, _ = xb.shape
    f32 = jnp.float32
    xc = dwconv(xb, conv_w, conv_b, CONV_B_LEFT, grid)
    xh = xc.reshape(bsz, T, HB, BW)
    rg = jax.nn.sigmoid((jnp.einsum('bthi,dhij->dbthj', xh, ga_w).reshape(N_DIR, bsz, T, WB)
                         + ga_b[:, None, None, :]).astype(f32))
    ig = jax.nn.sigmoid((jnp.einsum('bthi,dhij->dbthj', xh, gx_w).reshape(N_DIR, bsz, T, WB)
                         + gx_b[:, None, None, :]).astype(f32))
    log_a = -LRU_C * rg * jax.nn.softplus(-lam.astype(f32))[:, None, None, :]
    a = jnp.exp(log_a)
    u = jnp.sqrt(-jnp.expm1(2.0 * log_a)) * (ig * xc.astype(f32)[None])
    a_t = dir_time_major(a)
    u_t = dir_time_major(u)
    u_t = u_t.at[0].add(a_t[0] * h0.astype(f32))
    _, h = lax.associative_scan(_lin_combine, (a_t, u_t), axis=0)
    y = merge_dirs(h).astype(xb.dtype) * jax.nn.gelu(gb)
    return y, h[-1]


def block(x, cvec, s0, h0, grid, lp):
    mod = (jax.nn.silu(cvec) @ lp['w_mod'] + lp['b_mod'])[..., None, :]
    sh1, sc1, g1, sh2, sc2, g2 = jnp.split(mod, 6, axis=-1)
    h = rmsnorm(x, lp['ln1_g']) * (1.0 + sc1) + sh1
    proj = h @ lp['w_in']
    r, k, v, xw, xa, xg, xb, gb = jnp.split(proj, SPLITS, axis=-1)
    ya, s_fin = rwkv7_mix(r, k, v, xw, xa, xg, s0, lp['rw_w0'], lp['rw_w_up'], lp['rw_a0'], lp['rw_a_up'],
                          lp['rw_g_up'], lp['rw_k_k'], lp['rw_k_a'], lp['rw_r_k'], lp['rw_lnx_g'], lp['rw_lnx_b'])
    yb, h_fin = rglru_mix(xb, gb, h0, lp['lru_conv_w'], lp['lru_conv_b'], lp['lru_ga_w'], lp['lru_ga_b'],
                          lp['lru_gx_w'], lp['lru_gx_b'], lp['lru_lam'], grid)
    x = x + g1 * (jnp.concatenate([ya, yb], axis=-1) @ lp['w_out'])
    h = rmsnorm(x, lp['ln2_g']) * (1.0 + sc2) + sh2
    u = dwconv(h @ lp['w_up'], lp['ffn_conv_w'], lp['ffn_conv_b'], CONV_F_LEFT, grid)
    ua, ub = jnp.split(u, 2, axis=-1)
    x = x + g2 * ((jax.nn.gelu(ua) * ub) @ lp['w_down'])
    return x, s_fin, h_fin


def setup_inputs(seed: int = 0) -> dict:
    key = jax.random.key(seed)
    ks = iter(jax.random.split(key, 40))
    f32 = jnp.float32
    nrm = lambda shape, s: jax.random.normal(next(ks), shape, f32) * s
    L, D = DEPTH, D_MODEL
    x_prompt = nrm((BATCH, SEQ, D), 1.0)
    x_sample = nrm((DEC_BATCH, DEC_SEQ, D), 1.0)
    state_rwkv = nrm((DEC_BATCH, DEPTH, N_DIR, HA, HEAD_A, HEAD_A), 1.0)
    state_lru = nrm((DEC_BATCH, DEPTH, N_DIR, WB), 1.0)
    c = nrm((DEC_BATCH, D), 1.0)
    c_ctx = nrm((D,), 1.0)
    ln1_g = 1.0 + nrm((L, D), 0.02)
    w_mod = nrm((L, D, 6 * D), 0.5 * D ** -0.5)
    b_mod = nrm((L, 6 * D), 0.02)
    w_in = nrm((L, D, P_IN), D ** -0.5)
    rw_w0 = jax.random.uniform(next(ks), (L, N_DIR, WA), f32, -4.0, 2.0)
    rw_w_up = nrm((L, N_DIR, R_W, WA), 0.5 * R_W ** -0.5)
    rw_a0 = nrm((L, N_DIR, WA), 0.5)
    rw_a_up = nrm((L, N_DIR, R_A, WA), R_A ** -0.5)
    rw_g_up = nrm((L, R_G, WA), R_G ** -0.5)
    rw_k_k = 0.85 + nrm((L, WA), 0.1)
    rw_k_a = 1.0 + nrm((L, WA), 0.1)
    rw_r_k = nrm((L, HA, HEAD_A), 0.1)
    rw_lnx_g = 1.0 + nrm((L, WA), 0.02)
    rw_lnx_b = nrm((L, WA), 0.02)
    lru_conv_w = nrm((L, CONV_B, WB), CONV_B ** -0.5)
    lru_conv_b = nrm((L, WB), 0.02)
    lru_ga_w = nrm((L, N_DIR, HB, BW, BW), BW ** -0.5)
    lru_ga_b = nrm((L, N_DIR, WB), 0.02)
    lru_gx_w = nrm((L, N_DIR, HB, BW, BW), BW ** -0.5)
    lru_gx_b = nrm((L, N_DIR, WB), 0.02)
    a_pow = jax.random.uniform(next(ks), (L, N_DIR, WB), f32, 0.9, 0.999)
    p = a_pow ** (1.0 / LRU_C)
    lru_lam = jnp.log(p) - jnp.log1p(-p)
    w_out = nrm((L, D_MIX, D), D_MIX ** -0.5)
    ln2_g = 1.0 + nrm((L, D), 0.02)
    w_up = nrm((L, D, 2 * D_FF), D ** -0.5)
    ffn_conv_w = nrm((L, CONV_F, 2 * D_FF), CONV_F ** -0.5)
    ffn_conv_b = nrm((L, 2 * D_FF), 0.02)
    w_down = nrm((L, D_FF, D), D_FF ** -0.5)
    lnf_g = 1.0 + nrm((D,), 0.02)
    return {'x_prompt': x_prompt, 'x_sample': x_sample, 'state_rwkv': state_rwkv, 'state_lru': state_lru,
            'c': c, 'c_ctx': c_ctx, 'ln1_g': ln1_g, 'w_mod': w_mod, 'b_mod': b_mod, 'w_in': w_in,
            'rw_w0': rw_w0, 'rw_w_up': rw_w_up, 'rw_a0': rw_a0, 'rw_a_up': rw_a_up, 'rw_g_up': rw_g_up,
            'rw_k_k': rw_k_k, 'rw_k_a': rw_k_a, 'rw_r_k': rw_r_k, 'rw_lnx_g': rw_lnx_g, 'rw_lnx_b': rw_lnx_b,
            'lru_conv_w': lru_conv_w, 'lru_conv_b': lru_conv_b, 'lru_ga_w': lru_ga_w, 'lru_ga_b': lru_ga_b,
            'lru_gx_w': lru_gx_w, 'lru_gx_b': lru_gx_b, 'lru_lam': lru_lam, 'w_out': w_out, 'ln2_g': ln2_g,
            'w_up': w_up, 'ffn_conv_w': ffn_conv_w, 'ffn_conv_b': ffn_conv_b, 'w_down': w_down, 'lnf_g': lnf_g}


def reference(x_prompt, x_sample, state_rwkv, state_lru, c, c_ctx, ln1_g, w_mod, b_mod, w_in,
              rw_w0, rw_w_up, rw_a0, rw_a_up, rw_g_up, rw_k_k, rw_k_a, rw_r_k, rw_lnx_g, rw_lnx_b,
              lru_conv_w, lru_conv_b, lru_ga_w, lru_ga_b, lru_gx_w, lru_gx_b, lru_lam, w_out, ln2_g,
              w_up, ffn_conv_w, ffn_conv_b, w_down, lnf_g):
    bp = x_prompt.shape[0]
    yp = x_prompt
    ys = x_sample
    new_rwkv = []
    new_lru = []
    for l in range(DEPTH):
        lp = {'w_mod': w_mod[l], 'b_mod': b_mod[l], 'ln1_g': ln1_g[l], 'w_in': w_in[l],
              'rw_w0': rw_w0[l], 'rw_w_up': rw_w_up[l], 'rw_a0': rw_a0[l], 'rw_a_up': rw_a_up[l],
              'rw_g_up': rw_g_up[l], 'rw_k_k': rw_k_k[l], 'rw_k_a': rw_k_a[l], 'rw_r_k': rw_r_k[l],
              'rw_lnx_g': rw_lnx_g[l], 'rw_lnx_b': rw_lnx_b[l],
              'lru_conv_w': lru_conv_w[l], 'lru_conv_b': lru_conv_b[l], 'lru_ga_w': lru_ga_w[l],
              'lru_ga_b': lru_ga_b[l], 'lru_gx_w': lru_gx_w[l], 'lru_gx_b': lru_gx_b[l], 'lru_lam': lru_lam[l],
              'w_out': w_out[l], 'ln2_g': ln2_g[l], 'w_up': w_up[l], 'ffn_conv_w': ffn_conv_w[l],
              'ffn_conv_b': ffn_conv_b[l], 'w_down': w_down[l]}
        s0 = jnp.zeros((N_DIR, bp, HA, HEAD_A, HEAD_A), jnp.float32)
        h0 = jnp.zeros((N_DIR, bp, WB), jnp.float32)
        yp, s_fin, h_fin = block(yp, c_ctx, s0, h0, False, lp)
        new_rwkv.append(jnp.moveaxis(s_fin, 0, 1))
        new_lru.append(jnp.moveaxis(h_fin, 0, 1))
        ys, _, _ = block(ys, c, jnp.moveaxis(state_rwkv[:, l], 1, 0), jnp.moveaxis(state_lru[:, l], 1, 0), True, lp)
    y_prompt = rmsnorm(yp, lnf_g)
    y_sample = rmsnorm(ys, lnf_g)
    new_state_rwkv = jnp.stack(new_rwkv, axis=1).astype(x_prompt.dtype)
    new_state_lru = jnp.stack(new_lru, axis=1).astype(x_prompt.dtype)
    return (y_prompt, y_sample, new_state_rwkv, new_state_lru)
```

```python
import functools

import jax
import jax.numpy as jnp
from jax import lax
from jax.experimental import pallas as pl
from jax.experimental.pallas import tpu as pltpu

D_MODEL = 1024
DEPTH = 4
GRID_W = 64
WA = 512
HEAD_A = 64
HA = 8
WB = 512
HB = 8
BW = 64
R_W = 64
R_A = 64
R_G = 128
D_FF = 2816
CONV_B_LEFT = 2
CONV_F_LEFT = 1
LRU_C = 8.0
EPS = 1e-6
LNX_EPS = 64e-5
KK_EPS = 1e-12
DECAY_SCALE = 0.6065306597126334
SPLITS = (WA, 2 * WA, 3 * WA, 3 * WA + R_W, 3 * WA + R_W + R_A, 3 * WA + R_W + R_A + R_G,
          3 * WA + R_W + R_A + R_G + WB)

LANE = 128
VMEM_LIMIT = 48 * 1024 * 1024

WKV_CHUNK = 64
WKV_GROUP = 4
GW = WKV_GROUP * HEAD_A
NGROUP = HA // WKV_GROUP


def _mm_kernel(x_ref, w_ref, o_ref):
    o_ref[...] = jnp.dot(x_ref[...], w_ref[...], preferred_element_type=jnp.float32)


def _pick_tile(n, cap):
    if n <= cap:
        return n
    best = None
    for t in range(LANE, cap + 1, LANE):
        if n % t == 0:
            best = t
    assert best is not None, n
    return best


def mm(x, w, tm=512, tn_cap=1536):
    m, k = x.shape
    n = w.shape[1]
    tm = min(tm, m)
    assert m % tm == 0
    tn = _pick_tile(n, tn_cap)
    return pl.pallas_call(
        _mm_kernel,
        grid=(n // tn, m // tm),
        in_specs=[pl.BlockSpec((tm, k), lambda j, i: (i, 0)),
                  pl.BlockSpec((k, tn), lambda j, i: (0, j))],
        out_specs=pl.BlockSpec((tm, tn), lambda j, i: (i, j)),
        out_shape=jax.ShapeDtypeStruct((m, n), jnp.float32),
        compiler_params=pltpu.CompilerParams(
            dimension_semantics=("arbitrary", "arbitrary"), vmem_limit_bytes=VMEM_LIMIT),
    )(x, w)


def mm_bf16(x, w16):
    lead = x.shape[:-1]
    y = mm(x.reshape(-1, x.shape[-1]).astype(jnp.bfloat16), w16)
    return y.reshape(lead + (w16.shape[1],))


def _bd(x, bdmask):
    return jnp.where(bdmask, jnp.concatenate([x] * WKV_GROUP, axis=0), 0.0).astype(jnp.bfloat16)


def _dot(a, b):
    return jnp.dot(a.astype(jnp.bfloat16), b.astype(jnp.bfloat16), preferred_element_type=jnp.float32)


def _dot_nt(a, b):
    return lax.dot_general(a.astype(jnp.bfloat16), b.astype(jnp.bfloat16), (((1,), (1,)), ((), ())),
                           preferred_element_type=jnp.float32)


def _dot_tn(a, b):
    return lax.dot_general(a.astype(jnp.bfloat16), b.astype(jnp.bfloat16), (((0,), (0,)), ((), ())),
                           preferred_element_type=jnp.float32)


def _wkv_kernel(r_ref, v_ref, kk_ref, lw_ref, b_ref, kd_ref, s0_ref, y_ref, sfin_ref, s_scr):
    d = pl.program_id(0)
    c = pl.program_id(2)
    nc = pl.num_programs(2)
    C = WKV_CHUNK
    fwd = d == 0

    @pl.when(c == 0)
    def _():
        s_scr[...] = s0_ref[0, 0]

    sgn = 1 - 2 * d
    row = lax.broadcasted_iota(jnp.int32, (C, C), 0)
    col = lax.broadcasted_iota(jnp.int32, (C, C), 1)
    tri = jnp.where((row - col) * sgn >= 0, 1.0, 0.0)
    t_w = lax.broadcasted_iota(jnp.int32, (C, GW), 0)
    s_w = lax.broadcasted_iota(jnp.int32, (C, GW), 1) & (C - 1)
    ahead_w = (t_w - s_w) * sgn
    strict_w = ahead_w > 0
    incl_w = ahead_w >= 0
    eye_w = t_w == s_w
    bdmask = (lax.broadcasted_iota(jnp.int32, (GW, GW), 0) // HEAD_A
              == lax.broadcasted_iota(jnp.int32, (GW, GW), 1) // HEAD_A)

    for g in range(NGROUP):
        sl = slice(g * GW, (g + 1) * GW)
        lw = lw_ref[0, 0, :, sl]
        cum = jnp.dot(tri, lw, preferred_element_type=jnp.float32, precision=lax.Precision.HIGHEST)
        tot = jnp.where(fwd, cum[C - 1:C, :], cum[0:1, :])
        p = jnp.exp(cum)
        p_prev = jnp.exp(cum - lw)
        p_inv = jnp.exp(-cum)
        p_rem = jnp.exp(tot - cum)
        kk = kk_ref[0, :, sl]
        b = b_ref[0, 0, :, sl]
        kd = kd_ref[0, 0, :, sl]
        v = v_ref[0, :, sl]
        alpha = kk * p_prev
        beta = b * p_inv
        kappa = kd * p_inv
        rho = r_ref[0, :, sl] * p
        s_bd = s_scr[g]

        ar = jnp.concatenate([alpha, rho], axis=0)
        gmat = _dot_nt(ar, jnp.concatenate([_bd(beta, bdmask), _bd(kappa, bdmask)], axis=0))
        l_beta = jnp.where(strict_w, gmat[:C, :GW], 0.0)
        l_kappa = jnp.where(strict_w, gmat[:C, GW:], 0.0)
        q_beta = jnp.where(incl_w, gmat[C:, :GW], 0.0)
        q_kappa = jnp.where(incl_w, gmat[C:, GW:], 0.0)
        ws = _dot_nt(ar, s_bd)
        v_bd = _bd(v, bdmask)
        rhs = ws[:C] + _dot(l_kappa, v_bd)

        m_pow = -l_beta
        t_inv = jnp.where(eye_w, 1.0, 0.0) + m_pow
        m_pow = _dot(m_pow, _bd(m_pow, bdmask))
        n_sq = C.bit_length() - 2
        for i in range(n_sq):
            both = _dot(jnp.concatenate([t_inv, m_pow], axis=0), _bd(m_pow, bdmask))
            t_inv = t_inv + both[:C]
            m_pow = both[C:]
        t_inv = t_inv + _dot(t_inv, _bd(m_pow, bdmask))

        u = -_dot(t_inv, _bd(rhs, bdmask))
        y = ws[C:] + _dot(jnp.concatenate([q_beta, q_kappa], axis=1),
                          jnp.concatenate([_bd(u, bdmask), v_bd], axis=0))
        y_ref[0, 0, :, sl] = y
        ds = _dot_tn(jnp.concatenate([u, v], axis=0), jnp.concatenate([b * p_rem, kd * p_rem], axis=0))
        s_new = s_bd * jnp.exp(tot) + jnp.where(bdmask, ds, 0.0)
        s_scr[g] = s_new

        @pl.when(c == nc - 1)
        def _():
            sfin_ref[0, 0, g] = s_new


def wkv7(r, v, kk, lw, b, kd, s0_bd):
    bsz, t, _ = r.shape
    C = WKV_CHUNK
    nc = t // C
    assert t % C == 0

    def tmap(d, bi, c):
        return c + d * (nc - 1 - 2 * c)

    shared = pl.BlockSpec((1, C, WA), lambda d, bi, c: (bi, tmap(d, bi, c), 0))
    perdir = pl.BlockSpec((1, 1, C, WA), lambda d, bi, c: (d, bi, tmap(d, bi, c), 0))
    sspec = pl.BlockSpec((1, 1, NGROUP, GW, GW), lambda d, bi, c: (d, bi, 0, 0, 0))
    return pl.pallas_call(
        _wkv_kernel,
        grid=(2, bsz, nc),
        in_specs=[shared, shared, shared, perdir, perdir, perdir, sspec],
        out_specs=[perdir, sspec],
        out_shape=[jax.ShapeDtypeStruct((2, bsz, t, WA), jnp.float32),
                   jax.ShapeDtypeStruct((2, bsz, NGROUP, GW, GW), jnp.float32)],
        scratch_shapes=[pltpu.VMEM((NGROUP, GW, GW), jnp.float32)],
        compiler_params=pltpu.CompilerParams(
            dimension_semantics=("arbitrary", "arbitrary", "arbitrary"), vmem_limit_bytes=VMEM_LIMIT),
    )(r, v, kk, lw, b, kd, s0_bd)


def _to_bd(s):
    n_dir, bsz = s.shape[:2]
    s = s.reshape(n_dir, bsz, NGROUP, WKV_GROUP, HEAD_A, HEAD_A)
    eye = jnp.eye(WKV_GROUP, dtype=s.dtype)
    out = s[:, :, :, :, :, None, :] * eye[None, None, None, :, None, :, None]
    return out.reshape(n_dir, bsz, NGROUP, GW, GW)


def _from_bd(s):
    n_dir, bsz = s.shape[:2]
    s = s.reshape(n_dir, bsz, NGROUP, WKV_GROUP, HEAD_A, WKV_GROUP, HEAD_A)
    idx = jnp.arange(WKV_GROUP)
    out = s[:, :, :, idx, :, idx, :]
    out = jnp.moveaxis(out, 0, 3)
    return out.reshape(n_dir, bsz, HA, HEAD_A, HEAD_A)


LRU_TB = 256


def _lru_kernel(a_ref, u_ref, h0_ref, h_ref, carry):
    d = pl.program_id(0)
    tb = pl.program_id(2)
    nb = a_ref.shape[1]
    n = a_ref.shape[2]

    @pl.when(tb == 0)
    def _():
        carry[...] = h0_ref[0, 0]

    def body(i, hs):
        t = jnp.where(d == 0, i, n - 1 - i)
        out = []
        for bi in range(nb):
            h = a_ref[0, bi, pl.ds(t, 1), :] * hs[bi] + u_ref[0, bi, pl.ds(t, 1), :]
            h_ref[0, bi, pl.ds(t, 1), :] = h
            out.append(h)
        return tuple(out)

    hs = lax.fori_loop(0, n, body, tuple(carry[bi:bi + 1, :] for bi in range(nb)), unroll=8)
    for bi in range(nb):
        carry[bi:bi + 1, :] = hs[bi]


def lru_scan(a, u, h0, nb=4):
    n_dir, bsz, t, w = a.shape
    tb = min(LRU_TB, t)
    nt = t // tb

    def tmap(d, c):
        return c + d * (nt - 1 - 2 * c)

    spec = pl.BlockSpec((1, nb, tb, w), lambda d, bi, c: (d, bi, tmap(d, c), 0))
    return pl.pallas_call(
        _lru_kernel,
        grid=(n_dir, bsz // nb, nt),
        in_specs=[spec, spec, pl.BlockSpec((1, 1, nb, w), lambda d, bi, c: (d, bi, 0, 0))],
        out_specs=spec,
        out_shape=jax.ShapeDtypeStruct(a.shape, jnp.float32),
        scratch_shapes=[pltpu.VMEM((nb, w), jnp.float32)],
        compiler_params=pltpu.CompilerParams(
            dimension_semantics=("arbitrary", "arbitrary", "arbitrary"), vmem_limit_bytes=VMEM_LIMIT),
    )(a, u, h0.reshape(n_dir, bsz // nb, nb, w))


def rmsnorm(x, g):
    return x * lax.rsqrt(jnp.mean(x * x, axis=-1, keepdims=True) + EPS) * g


def dwconv(x, w, b, left, grid):
    bsz, L, C = x.shape
    K = w.shape[0]
    if grid:
        x = x.reshape(bsz * (L // GRID_W), GRID_W, C)
    n = x.shape[1]
    xp = jnp.pad(x, ((0, 0), (left, K - 1 - left), (0, 0)))
    y = b + xp[:, 0:n, :] * w[0]
    for j in range(1, K):
        y = y + xp[:, j:j + n, :] * w[j]
    return y.reshape(bsz, L, C)


def _heads(t):
    return t.reshape(t.shape[:-1] + (HA, HEAD_A))


def rwkv7_mix(r, k, v, xw, xa, xg, s0_bd, lp):
    bsz, T, _ = r.shape
    xw_t = jnp.tanh(xw)
    logit_w = lp['rw_w0'][:, None, None, :] + jnp.stack(
        [mm_bf16(xw_t, lp['rw_w_up16'][d]) for d in range(2)])
    lw = -DECAY_SCALE * jax.nn.sigmoid(logit_w)
    a = jax.nn.sigmoid(lp['rw_a0'][:, None, None, :] + jnp.stack(
        [mm_bf16(xa, lp['rw_a_up16'][d]) for d in range(2)]))
    g = mm_bf16(jax.nn.sigmoid(xg), lp['rw_g_up16'])
    kk = _heads(k * lp['rw_k_k'])
    kk = (kk * lax.rsqrt(jnp.sum(kk * kk, axis=-1, keepdims=True) + KK_EPS)).reshape(bsz, T, WA)
    k_dir = k[None] * (1.0 + (a - 1.0) * lp['rw_k_a'])
    y2, s_fin = wkv7(r, v, kk, lw, kk[None] * a, k_dir, s0_bd)
    o = _heads(y2[0] + y2[1])
    mu = jnp.mean(o, axis=-1, keepdims=True)
    var = jnp.mean(jnp.square(o - mu), axis=-1, keepdims=True)
    o = ((o - mu) * lax.rsqrt(var + LNX_EPS)).reshape(bsz, T, WA) * lp['rw_lnx_g'] + lp['rw_lnx_b']
    rh, vh, kdh = _heads(r), _heads(v), _heads(k_dir)
    bonus = jnp.sum(rh[None] * kdh * lp['rw_r_k'], axis=(0, -1))[..., None] * vh
    out = (o + bonus.reshape(bsz, T, WA)) * g
    return out, s_fin


def rglru_mix(xb, gb, h0, lp, grid):
    bsz, T, _ = xb.shape
    xc = dwconv(xb, lp['lru_conv_w'], lp['lru_conv_b'], CONV_B_LEFT, grid)
    gates = mm_bf16(xc, lp['lru_gate_w16'])
    gates = jnp.moveaxis(gates.reshape(bsz, T, 2, 2, WB), (2, 3), (0, 1))
    rg = jax.nn.sigmoid(gates[0] + lp['lru_ga_b'][:, None, None, :])
    ig = jax.nn.sigmoid(gates[1] + lp['lru_gx_b'][:, None, None, :])
    log_a = -LRU_C * rg * jax.nn.softplus(-lp['lru_lam'])[:, None, None, :]
    a = jnp.exp(log_a)
    u = jnp.sqrt(-jnp.expm1(2.0 * log_a)) * (ig * xc[None])
    h = lru_scan(a, u, h0, nb=4)
    y = (h[0] + h[1]) * jax.nn.gelu(gb)
    h_fin = jnp.stack([h[0, :, T - 1, :], h[1, :, 0, :]])
    return y, h_fin


def block(x, mod, s0_bd, h0, grid, lp):
    sh1, sc1, g1, sh2, sc2, g2 = jnp.split(mod[..., None, :], 6, axis=-1)
    h = rmsnorm(x, lp['ln1_g']) * (1.0 + sc1) + sh1
    proj = mm_bf16(h, lp['w_in16'])
    r, k, v, xw, xa, xg, xb, gb = jnp.split(proj, SPLITS, axis=-1)
    ya, s_fin = rwkv7_mix(r, k, v, xw, xa, xg, s0_bd, lp)
    yb, h_fin = rglru_mix(xb, gb, h0, lp, grid)
    x = x + g1 * mm_bf16(jnp.concatenate([ya, yb], axis=-1), lp['w_out16'])
    h = rmsnorm(x, lp['ln2_g']) * (1.0 + sc2) + sh2
    u = dwconv(mm_bf16(h, lp['w_up16']), lp['ffn_conv_w'], lp['ffn_conv_b'], CONV_F_LEFT, grid)
    ua, ub = jnp.split(u, 2, axis=-1)
    x = x + g2 * mm_bf16(jax.nn.gelu(ua) * ub, lp['w_down16'])
    return x, s_fin, h_fin


def _block_diag(w):
    eye = jnp.eye(HB, dtype=w.dtype)
    return (w[:, :, None, :] * eye[:, None, :, None]).reshape(WB, WB)


def kernel(x_prompt, x_sample, state_rwkv, state_lru, c, c_ctx, ln1_g, w_mod, b_mod, w_in, rw_w0, rw_w_up, rw_a0, rw_a_up, rw_g_up, rw_k_k, rw_k_a, rw_r_k, rw_lnx_g, rw_lnx_b, lru_conv_w, lru_conv_b, lru_ga_w, lru_ga_b, lru_gx_w, lru_gx_b, lru_lam, w_out, ln2_g, w_up, ffn_conv_w, ffn_conv_b, w_down, lnf_g):
    bf16 = jnp.bfloat16
    bp = x_prompt.shape[0]
    bs = x_sample.shape[0]
    yp, ys = x_prompt, x_sample
    cc = jnp.concatenate([c, c_ctx[None], jnp.zeros((8 - bs - 1, D_MODEL), c.dtype)], axis=0)
    cc = jax.nn.silu(cc)
    new_rwkv, new_lru = [], []
    zero_s = jnp.zeros((2, bp, NGROUP, GW, GW), jnp.float32)
    zero_h = jnp.zeros((2, bp, WB), jnp.float32)
    for l in range(DEPTH):
        gate_w = jnp.concatenate([_block_diag(lru_ga_w[l, 0]), _block_diag(lru_ga_w[l, 1]),
                                  _block_diag(lru_gx_w[l, 0]), _block_diag(lru_gx_w[l, 1])], axis=1)
        lp = {'ln1_g': ln1_g[l], 'w_in16': w_in[l].astype(bf16),
              'rw_w0': rw_w0[l], 'rw_w_up16': rw_w_up[l].astype(bf16), 'rw_a0': rw_a0[l],
              'rw_a_up16': rw_a_up[l].astype(bf16), 'rw_g_up16': rw_g_up[l].astype(bf16),
              'rw_k_k': rw_k_k[l], 'rw_k_a': rw_k_a[l], 'rw_r_k': rw_r_k[l],
              'rw_lnx_g': rw_lnx_g[l], 'rw_lnx_b': rw_lnx_b[l],
              'lru_conv_w': lru_conv_w[l], 'lru_conv_b': lru_conv_b[l], 'lru_gate_w16': gate_w.astype(bf16),
              'lru_ga_b': lru_ga_b[l], 'lru_gx_b': lru_gx_b[l], 'lru_lam': lru_lam[l],
              'w_out16': w_out[l].astype(bf16), 'ln2_g': ln2_g[l], 'w_up16': w_up[l].astype(bf16),
              'ffn_conv_w': ffn_conv_w[l], 'ffn_conv_b': ffn_conv_b[l], 'w_down16': w_down[l].astype(bf16)}
        mod = mm(cc, w_mod[l], tm=8) + b_mod[l]
        yp, s_fin, h_fin = block(yp, mod[bs], zero_s, zero_h, False, lp)
        new_rwkv.append(jnp.moveaxis(_from_bd(s_fin), 0, 1))
        new_lru.append(jnp.moveaxis(h_fin, 0, 1))
        s0 = _to_bd(jnp.moveaxis(state_rwkv[:, l], 1, 0))
        ys, _, _ = block(ys, mod[:bs], s0, jnp.moveaxis(state_lru[:, l], 1, 0), True, lp)
    y_prompt = rmsnorm(yp, lnf_g)
    y_sample = rmsnorm(ys, lnf_g)
    return (y_prompt, y_sample, jnp.stack(new_rwkv, axis=1), jnp.stack(new_lru, axis=1))
```

```python
import functools

import jax
import jax.numpy as jnp
from jax import lax
from jax.experimental import pallas as pl
from jax.experimental.pallas import tpu as pltpu

D_MODEL = 1024
DEPTH = 4
GRID_W = 64
WA = 512
HEAD_A = 64
HA = 8
WB = 512
HB = 8
BW = 64
R_W = 64
R_A = 64
R_G = 128
D_FF = 2816
CONV_B_LEFT = 2
CONV_F_LEFT = 1
LRU_C = 8.0
EPS = 1e-6
LNX_EPS = 64e-5
KK_EPS = 1e-12
DECAY_SCALE = 0.6065306597126334
SPLITS = (WA, 2 * WA, 3 * WA, 3 * WA + R_W, 3 * WA + R_W + R_A, 3 * WA + R_W + R_A + R_G,
          3 * WA + R_W + R_A + R_G + WB)

LANE = 128
VMEM_LIMIT = 48 * 1024 * 1024

WKV_CHUNK = 64
WKV_GROUP = 4
GW = WKV_GROUP * HEAD_A
NGROUP = HA // WKV_GROUP


def _mm_kernel(x_ref, w_ref, o_ref):
    o_ref[...] = jnp.dot(x_ref[...], w_ref[...], preferred_element_type=jnp.float32)


def _pick_tile(n, cap):
    if n <= cap:
        return n
    best = None
    for t in range(LANE, cap + 1, LANE):
        if n % t == 0:
            best = t
    assert best is not None, n
    return best


def mm(x, w, tm=512, tn_cap=1536):
    m, k = x.shape
    n = w.shape[1]
    tm = min(tm, m)
    assert m % tm == 0
    tn = _pick_tile(n, tn_cap)
    return pl.pallas_call(
        _mm_kernel,
        grid=(n // tn, m // tm),
        in_specs=[pl.BlockSpec((tm, k), lambda j, i: (i, 0)),
                  pl.BlockSpec((k, tn), lambda j, i: (0, j))],
        out_specs=pl.BlockSpec((tm, tn), lambda j, i: (i, j)),
        out_shape=jax.ShapeDtypeStruct((m, n), jnp.float32),
        compiler_params=pltpu.CompilerParams(
            dimension_semantics=("arbitrary", "arbitrary"), vmem_limit_bytes=VMEM_LIMIT),
        name="mm",
    )(x, w)


def _bd(x, bdmask):
    return jnp.where(bdmask, jnp.concatenate([x] * WKV_GROUP, axis=0), 0.0).astype(jnp.bfloat16)


def _dot(a, b):
    return jnp.dot(a.astype(jnp.bfloat16), b.astype(jnp.bfloat16), preferred_element_type=jnp.float32)


def _dot_nt(a, b):
    return lax.dot_general(a.astype(jnp.bfloat16), b.astype(jnp.bfloat16), (((1,), (1,)), ((), ())),
                           preferred_element_type=jnp.float32)


def _dot_tn(a, b):
    return lax.dot_general(a.astype(jnp.bfloat16), b.astype(jnp.bfloat16), (((0,), (0,)), ((), ())),
                           preferred_element_type=jnp.float32)


def _split_bf16(x, n):
    parts = []
    for _ in range(n):
        p = x.astype(jnp.bfloat16)
        parts.append(p)
        x = x - p.astype(jnp.float32)
    return parts


def _wkv_masks(sgns):
    C = WKV_CHUNK
    row = lax.broadcasted_iota(jnp.int32, (C, C), 0)
    col = lax.broadcasted_iota(jnp.int32, (C, C), 1)
    t_w = lax.broadcasted_iota(jnp.int32, (C, GW), 0)
    s_w = lax.broadcasted_iota(jnp.int32, (C, GW), 1) & (C - 1)
    bdmask = (lax.broadcasted_iota(jnp.int32, (GW, GW), 0) // HEAD_A
              == lax.broadcasted_iota(jnp.int32, (GW, GW), 1) // HEAD_A)
    return dict(
        eye_w=t_w == s_w, bdmask=bdmask,
        tri={s: jnp.where((row - col) * s >= 0, 1.0, 0.0).astype(jnp.bfloat16) for s in set(sgns)},
        strict_w={s: (t_w - s_w) * s > 0 for s in set(sgns)},
        incl_w={s: (t_w - s_w) * s >= 0 for s in set(sgns)})


def _wkv_prepare(chains, masks, out):
    C = WKV_CHUNK
    n = len(chains)
    bf16 = jnp.bfloat16
    bdmask, eye_w = masks['bdmask'], masks['eye_w']
    ones_bd = jnp.where(bdmask, 1.0, 0.0).astype(bf16)
    sgns = [ch[0] for ch in chains]
    each = lambda f: [f(i) for i in range(n)]
    r, k, v, lw, a, k_k, k_a = (each(lambda i, j=j: chains[i][j]) for j in range(1, 8))

    kk = each(lambda i: k[i] * k_k[i])
    ssq = each(lambda i: jnp.dot(jnp.concatenate(_split_bf16(kk[i] * kk[i], 2), axis=0), ones_bd,
                                 preferred_element_type=jnp.float32))
    cum = each(lambda i: jnp.dot(masks['tri'][sgns[i]], jnp.concatenate(_split_bf16(lw[i], 3), axis=1),
                                 preferred_element_type=jnp.float32))
    yield
    ssq = each(lambda i: ssq[i][:C] + ssq[i][C:])
    cum = each(lambda i: cum[i][:, :GW] + cum[i][:, GW:2 * GW] + cum[i][:, 2 * GW:])
    kk = each(lambda i: kk[i] * lax.rsqrt(ssq[i] + KK_EPS))
    b = each(lambda i: kk[i] * a[i])
    kd = each(lambda i: k[i] * (1.0 + (a[i] - 1.0) * k_a[i]))
    tot = each(lambda i: cum[i][C - 1:C, :] if sgns[i] > 0 else cum[i][0:1, :])
    p_inv = each(lambda i: jnp.exp(-cum[i]))
    p_rem = each(lambda i: jnp.exp(tot[i] - cum[i]))
    ar = each(lambda i: jnp.concatenate([kk[i] * jnp.exp(cum[i] - lw[i]), r[i] * jnp.exp(cum[i])],
                                        axis=0).astype(bf16))
    bk_bd = each(lambda i: jnp.concatenate([_bd(b[i] * p_inv[i], bdmask), _bd(kd[i] * p_inv[i], bdmask)], axis=0))
    gmat = each(lambda i: _dot_nt(ar[i], bk_bd[i]))
    yield
    strict_w = each(lambda i: masks['strict_w'][sgns[i]])
    incl_w = each(lambda i: masks['incl_w'][sgns[i]])
    l_kappa = each(lambda i: jnp.where(strict_w[i], gmat[i][:C, GW:], 0.0))
    q_both = each(lambda i: jnp.concatenate([jnp.where(incl_w[i], gmat[i][C:, :GW], 0.0),
                                             jnp.where(incl_w[i], gmat[i][C:, GW:], 0.0)], axis=1).astype(bf16))
    m_pow = each(lambda i: jnp.where(strict_w[i], -gmat[i][:C, :GW], 0.0))
    t_inv = each(lambda i: jnp.where(eye_w, 1.0, 0.0) + m_pow[i])
    lkv = each(lambda i: _dot(l_kappa[i], _bd(v[i], bdmask)))
    m_pow = each(lambda i: _dot(m_pow[i], _bd(m_pow[i], bdmask)))
    yield
    for _ in range(C.bit_length() - 3):
        both = each(lambda i: _dot(jnp.concatenate([t_inv[i], m_pow[i]], axis=0), _bd(m_pow[i], bdmask)))
        yield
        t_inv = each(lambda i: t_inv[i] + both[i][:C])
        m_pow = each(lambda i: both[i][C:])
    t_inv = each(lambda i: t_inv[i] + _dot(t_inv[i], _bd(m_pow[i], bdmask)))
    yield
    for i in range(n):
        out.append((ar[i], t_inv[i].astype(bf16), lkv[i], q_both[i], v[i].astype(bf16),
                    jnp.concatenate([b[i] * p_rem[i], kd[i] * p_rem[i]], axis=0).astype(bf16), jnp.exp(tot[i])))


def _wkv_advance(prepared, states, masks, out):
    C = WKV_CHUNK
    n = len(prepared)
    bdmask = masks['bdmask']
    each = lambda f: [f(i) for i in range(n)]
    ar, t_inv, lkv, q_both, v, bkp, etot = (each(lambda i, j=j: prepared[i][j]) for j in range(7))
    ws = each(lambda i: _dot_nt(ar[i], states[i]))
    yield
    u = each(lambda i: -_dot(t_inv[i], _bd(ws[i][:C] + lkv[i], bdmask)))
    yield
    v_bd = each(lambda i: _bd(v[i].astype(jnp.float32), bdmask))
    y = each(lambda i: ws[i][C:] + _dot(q_both[i], jnp.concatenate([_bd(u[i], bdmask), v_bd[i]], axis=0)))
    ds = each(lambda i: _dot_tn(jnp.concatenate([u[i].astype(jnp.bfloat16), v[i]], axis=0), bkp[i]))
    yield
    for i in range(n):
        out.append((y[i], states[i] * etot[i] + jnp.where(bdmask, ds[i], 0.0)))


def _interleave(*gens):
    live = list(gens)
    while live:
        for g in list(live):
            try:
                next(g)
            except StopIteration:
                live.remove(g)


WKV_NB = 2


def _wkv_kernel(*refs, zero_init, emit_state):
    proj_refs = refs[:8]
    wup_ref, aup_ref, vec_ref = refs[8:11]
    pos = 11
    if not zero_init:
        s0_ref = refs[pos]
        pos += 1
    y_refs = refs[pos:pos + 2]
    pos += 2
    if emit_state:
        sfin_ref = refs[pos]
        pos += 1
    s_scr = refs[pos]
    c = pl.program_id(1)
    last = pl.num_programs(1) - 1

    @pl.when(c == 0)
    def _():
        if zero_init:
            s_scr[...] = jnp.zeros_like(s_scr)
        else:
            s_scr[...] = s0_ref[...]

    chain_ids = [(d, bi, g) for bi in range(WKV_NB) for d in range(2) for g in range(NGROUP)]
    raw = []
    for d, bi, g in chain_ids:
        r_ref, k_ref, v_ref, x_ref = proj_refs[4 * d:4 * d + 4]
        sl = slice(g * GW, (g + 1) * GW)
        x = x_ref[bi]
        lw = -DECAY_SCALE * jax.nn.sigmoid(vec_ref[d:d + 1, sl] + _dot(jnp.tanh(x), wup_ref[d, :, sl]))
        a = jax.nn.sigmoid(vec_ref[2 + d:3 + d, sl] + _dot(x, aup_ref[d, :, sl]))
        raw.append((1 - 2 * d, r_ref[bi, :, sl], k_ref[bi, :, sl], v_ref[bi, :, sl], lw, a, vec_ref[4:5, sl],
                    vec_ref[5:6, sl]))
    masks = _wkv_masks([1, -1])
    prepared, advanced = [], []
    _interleave(_wkv_prepare(raw, masks, prepared))
    _interleave(_wkv_advance(prepared, [s_scr[d, bi, g] for d, bi, g in chain_ids], masks, advanced))

    for (d, bi, g), (y, s_new) in zip(chain_ids, advanced):
        y_refs[d][bi, :, g * GW:(g + 1) * GW] = y
        s_scr[d, bi, g] = s_new
    if emit_state:
        @pl.when(c == last)
        def _():
            for (d, bi, g), (_, s_new) in zip(chain_ids, advanced):
                sfin_ref[d, bi, g] = s_new


def wkv7(proj, wup_pad, aup_pad, vecs, s0_bd=None, emit_state=True):
    bsz, t, _ = proj.shape
    C = WKV_CHUNK
    nc = t // C
    assert t % C == 0 and bsz % WKV_NB == 0
    zero_init = s0_bd is None

    def chunk(d, c):
        return c if d == 0 else nc - 1 - c

    in_specs, args = [], []
    for d in range(2):
        for j in range(3):
            in_specs.append(pl.BlockSpec((WKV_NB, C, WA), functools.partial(lambda bi, c, d, j: (bi, chunk(d, c), j), d=d, j=j)))
        in_specs.append(pl.BlockSpec((WKV_NB, C, GW), functools.partial(lambda bi, c, d: (bi, chunk(d, c), COL_X), d=d)))
        args += [proj] * 4
    in_specs += [pl.BlockSpec((2, GW, WA), lambda bi, c: (0, 0, 0)), pl.BlockSpec((2, GW, WA), lambda bi, c: (0, 0, 0)),
                 pl.BlockSpec((8, WA), lambda bi, c: (0, 0))]
    args += [wup_pad, aup_pad, vecs]
    sspec = pl.BlockSpec((2, WKV_NB, NGROUP, GW, GW), lambda bi, c: (0, bi, 0, 0, 0))
    if not zero_init:
        in_specs.append(sspec)
        args.append(s0_bd)
    out_specs = [pl.BlockSpec((WKV_NB, C, WA), functools.partial(lambda bi, c, d: (bi, chunk(d, c), 0), d=d))
                 for d in range(2)]
    out_shape = [jax.ShapeDtypeStruct((bsz, t, WA), jnp.float32)] * 2
    if emit_state:
        out_specs.append(sspec)
        out_shape.append(jax.ShapeDtypeStruct((2, bsz, NGROUP, GW, GW), jnp.float32))
    return pl.pallas_call(
        functools.partial(_wkv_kernel, zero_init=zero_init, emit_state=emit_state),
        grid=(bsz // WKV_NB, nc),
        in_specs=in_specs,
        out_specs=out_specs,
        out_shape=out_shape,
        scratch_shapes=[pltpu.VMEM((2, WKV_NB, NGROUP, GW, GW), jnp.float32)],
        compiler_params=pltpu.CompilerParams(
            dimension_semantics=("arbitrary", "arbitrary"), vmem_limit_bytes=VMEM_LIMIT),
        name="wkv7",
    )(*args)


def _to_bd(s):
    n_dir, bsz = s.shape[:2]
    s = s.reshape(n_dir, bsz, NGROUP, WKV_GROUP, HEAD_A, HEAD_A)
    eye = jnp.eye(WKV_GROUP, dtype=s.dtype)
    out = s[:, :, :, :, :, None, :] * eye[None, None, None, :, None, :, None]
    return out.reshape(n_dir, bsz, NGROUP, GW, GW)


def _from_bd(s):
    n_dir, bsz = s.shape[:2]
    s = s.reshape(n_dir, bsz, NGROUP, WKV_GROUP, HEAD_A, WKV_GROUP, HEAD_A)
    idx = jnp.arange(WKV_GROUP)
    out = s[:, :, :, idx, :, idx, :]
    out = jnp.moveaxis(out, 0, 3)
    return out.reshape(n_dir, bsz, HA, HEAD_A, HEAD_A)


TOK_TILE = 512
FFN_TILE = 1024
FFN_CHUNK = 256
LRU_TB = 256
COL_XB, COL_GB, COL_X = 3, 4, 10


def _rms(x):
    return x * lax.rsqrt(jnp.mean(x * x, axis=-1, keepdims=True) + EPS)


def _head_sums(x, ones_bd16):
    n = x.shape[0]
    s = jnp.dot(jnp.concatenate(_split_bf16(x, 2), axis=0), ones_bd16, preferred_element_type=jnp.float32)
    return s[:n] + s[n:]


def _in_kernel(x_ref, mod_ref, g_ref, w_ref, o_ref):
    h = _rms(x_ref[...]) * g_ref[...] * (1.0 + mod_ref[0, 1:2, :]) + mod_ref[0, 0:1, :]
    o_ref[...] = jnp.dot(h.astype(jnp.bfloat16), w_ref[...], preferred_element_type=jnp.float32)


def in_proj(x, mod6, ln_g, w16, tiles_per_mod):
    n, dm = x.shape
    p = w16.shape[1]
    return pl.pallas_call(
        _in_kernel,
        grid=(n // TOK_TILE,),
        in_specs=[pl.BlockSpec((TOK_TILE, dm), lambda i: (i, 0)),
                  pl.BlockSpec((1, 6, dm), lambda i: (i // tiles_per_mod, 0, 0)),
                  pl.BlockSpec((1, dm), lambda i: (0, 0)),
                  pl.BlockSpec((dm, p), lambda i: (0, 0))],
        out_specs=pl.BlockSpec((TOK_TILE, p), lambda i: (i, 0)),
        out_shape=jax.ShapeDtypeStruct((n, p), jnp.float32),
        compiler_params=pltpu.CompilerParams(dimension_semantics=("arbitrary",), vmem_limit_bytes=VMEM_LIMIT),
        name="in_proj",
    )(x, mod6, ln_g.reshape(1, dm), w16)


def _post_kernel(x_ref, mod_ref, y0_ref, y1_ref, r_ref, k_ref, v_ref, xx_ref, gb_ref, h0_ref, h1_ref,
                 aup_ref, gup_ref, vec_ref, wo_ref, o_ref):
    bf16 = jnp.bfloat16
    hr = lax.broadcasted_iota(jnp.int32, (WA, WA), 0) // HEAD_A
    hc = lax.broadcasted_iota(jnp.int32, (WA, WA), 1) // HEAD_A
    ones_bd16 = jnp.where(hr == hc, 1.0, 0.0).astype(bf16)
    xx = xx_ref[...]
    xx16 = xx.astype(bf16)
    a_sum = (jax.nn.sigmoid(vec_ref[0:1, :] + jnp.dot(xx16, aup_ref[0], preferred_element_type=jnp.float32))
             + jax.nn.sigmoid(vec_ref[1:2, :] + jnp.dot(xx16, aup_ref[1], preferred_element_type=jnp.float32)))
    gate = jnp.dot(jax.nn.sigmoid(xx).astype(bf16), gup_ref[...], preferred_element_type=jnp.float32)
    k = k_ref[...]
    k_both = k * (2.0 + (a_sum - 2.0) * vec_ref[2:3, :])
    bonus = _head_sums(r_ref[...] * k_both * vec_ref[3:4, :], ones_bd16) * v_ref[...]
    o = y0_ref[...] + y1_ref[...]
    mu = _head_sums(o, ones_bd16) * (1.0 / HEAD_A)
    oc = o - mu
    var = _head_sums(oc * oc, ones_bd16) * (1.0 / HEAD_A)
    ya = (oc * lax.rsqrt(var + LNX_EPS) * vec_ref[4:5, :] + vec_ref[5:6, :] + bonus) * gate
    yb = (h0_ref[...] + h1_ref[...]) * jax.nn.gelu(gb_ref[...])
    mix = (jnp.dot(ya.astype(bf16), wo_ref[:WA, :], preferred_element_type=jnp.float32)
           + jnp.dot(yb.astype(bf16), wo_ref[WA:, :], preferred_element_type=jnp.float32))
    o_ref[...] = x_ref[...] + mod_ref[0, 2:3, :] * mix


def post_mix(x, mod6, y0, y1, proj, h0, h1, aup_pad, gup_pad, vecs, wo16, tiles_per_mod):
    n, dm = x.shape
    tm = TOK_TILE
    tok = lambda w, j: pl.BlockSpec((tm, w), lambda i: (i, j))
    full = lambda a: pl.BlockSpec(a.shape, lambda i: (0,) * a.ndim)
    return pl.pallas_call(
        _post_kernel,
        grid=(n // tm,),
        in_specs=[tok(dm, 0), pl.BlockSpec((1, 6, dm), lambda i: (i // tiles_per_mod, 0, 0)),
                  tok(WA, 0), tok(WA, 0), tok(WA, 0), tok(WA, 1), tok(WA, 2), tok(GW, COL_X), tok(WA, COL_GB),
                  tok(WA, 0), tok(WA, 0), full(aup_pad), full(gup_pad), full(vecs), full(wo16)],
        out_specs=tok(dm, 0),
        out_shape=jax.ShapeDtypeStruct((n, dm), jnp.float32),
        compiler_params=pltpu.CompilerParams(dimension_semantics=("arbitrary",), vmem_limit_bytes=VMEM_LIMIT),
        name="post_mix",
    )(x, mod6, y0, y1, proj, proj, proj, proj, proj, h0, h1, aup_pad, gup_pad, vecs, wo16)


def _shifted(u, shift, seq_len):
    n = u.shape[0]
    pos = lax.broadcasted_iota(jnp.int32, u.shape, 0) & (seq_len - 1)
    rolled = pltpu.roll(u, (-shift) % n, 0)
    ok = (pos + shift >= 0) & (pos + shift < seq_len)
    return jnp.where(ok, rolled, 0.0)


def _ffn_kernel(x_ref, mod_ref, g_ref, wa_ref, wb_ref, cwa_ref, cwb_ref, cba_ref, cbb_ref, wd_ref, gf_ref, o_ref,
                h_scr, acc_scr, *, seq_len, final_norm):
    f = pl.program_id(1)

    @pl.when(f == 0)
    def _():
        h = _rms(x_ref[...]) * g_ref[...] * (1.0 + mod_ref[0, 4:5, :]) + mod_ref[0, 3:4, :]
        h_scr[...] = h.astype(jnp.bfloat16)
        acc_scr[...] = jnp.zeros_like(acc_scr)

    def conv(w_ref, cw_ref, cb_ref):
        u = jnp.dot(h_scr[...], w_ref[...], preferred_element_type=jnp.float32)
        return (cb_ref[...] + _shifted(u, -1, seq_len) * cw_ref[0:1, :] + u * cw_ref[1:2, :]
                + _shifted(u, 1, seq_len) * cw_ref[2:3, :])

    hid = jax.nn.gelu(conv(wa_ref, cwa_ref, cba_ref)) * conv(wb_ref, cwb_ref, cbb_ref)
    acc_scr[...] += jnp.dot(hid.astype(jnp.bfloat16), wd_ref[...], preferred_element_type=jnp.float32)

    @pl.when(f == pl.num_programs(1) - 1)
    def _():
        out = x_ref[...] + mod_ref[0, 5:6, :] * acc_scr[...]
        if final_norm:
            out = _rms(out) * gf_ref[...]
        o_ref[...] = out


def ffn(x, mod6, ln_g, wup16, conv_w, conv_b, wdown16, lnf_g, tiles_per_mod, seq_len, final_norm):
    n, dm = x.shape
    tm, fc = FFN_TILE, FFN_CHUNK
    nf = D_FF // fc
    assert D_FF % fc == 0 and tm % seq_len == 0 and n % tm == 0
    cb = conv_b.reshape(1, 2 * D_FF)
    return pl.pallas_call(
        functools.partial(_ffn_kernel, seq_len=seq_len, final_norm=final_norm),
        grid=(n // tm, nf),
        in_specs=[pl.BlockSpec((tm, dm), lambda i, f: (i, 0)),
                  pl.BlockSpec((1, 6, dm), lambda i, f: (i // tiles_per_mod, 0, 0)),
                  pl.BlockSpec((1, dm), lambda i, f: (0, 0)),
                  pl.BlockSpec((dm, fc), lambda i, f: (0, f)), pl.BlockSpec((dm, fc), lambda i, f: (0, nf + f)),
                  pl.BlockSpec((3, fc), lambda i, f: (0, f)), pl.BlockSpec((3, fc), lambda i, f: (0, nf + f)),
                  pl.BlockSpec((1, fc), lambda i, f: (0, f)), pl.BlockSpec((1, fc), lambda i, f: (0, nf + f)),
                  pl.BlockSpec((fc, dm), lambda i, f: (f, 0)),
                  pl.BlockSpec((1, dm), lambda i, f: (0, 0))],
        out_specs=pl.BlockSpec((tm, dm), lambda i, f: (i, 0)),
        out_shape=jax.ShapeDtypeStruct((n, dm), jnp.float32),
        scratch_shapes=[pltpu.VMEM((tm, dm), jnp.bfloat16), pltpu.VMEM((tm, dm), jnp.float32)],
        compiler_params=pltpu.CompilerParams(
            dimension_semantics=("arbitrary", "arbitrary"), vmem_limit_bytes=VMEM_LIMIT),
        name="ffn",
    )(x, mod6, ln_g.reshape(1, dm), wup16, wup16, conv_w, conv_w, cb, cb, wdown16, lnf_g.reshape(1, dm))


def _lru_scan_block(a, u, carry, reverse):
    n = a.shape[0]
    row = lax.broadcasted_iota(jnp.int32, a.shape, 0)
    s = 1
    while s < n:
        ok = (row < n - s) if reverse else (row >= s)
        shift = (n - s) if reverse else s
        a_prev = jnp.where(ok, pltpu.roll(a, shift, 0), 1.0)
        u_prev = jnp.where(ok, pltpu.roll(u, shift, 0), 0.0)
        u = a * u_prev + u
        a = a * a_prev
        s *= 2
    return a * carry + u


def _lru_kernel(*refs, seq_len, emit_state):
    xb_refs = refs[0:2]
    gw_ref, cw_ref, vec_ref, h0_ref = refs[2:6]
    h_refs = refs[6:8]
    pos = 8
    if emit_state:
        hfin_ref = refs[pos]
        pos += 1
    carry = refs[pos]
    c = pl.program_id(1)

    @pl.when(c == 0)
    def _():
        carry[...] = h0_ref[0]

    for d in range(2):
        xb = xb_refs[d][0]
        xc = vec_ref[0:1, :]
        for j in range(4):
            xc = xc + _shifted(xb, j - CONV_B_LEFT, seq_len) * cw_ref[j:j + 1, :]
        gates = jnp.dot(xc.astype(jnp.bfloat16), gw_ref[d], preferred_element_type=jnp.float32)
        rg = jax.nn.sigmoid(gates[:, :WB] + vec_ref[1 + d:2 + d, :])
        ig = jax.nn.sigmoid(gates[:, WB:] + vec_ref[3 + d:4 + d, :])
        log_a = -LRU_C * rg * vec_ref[5 + d:6 + d, :]
        a = jnp.exp(log_a)
        u = jnp.sqrt(-jnp.tanh(log_a) * (a * a + 1.0)) * (ig * xc)
        h = _lru_scan_block(a, u, carry[d:d + 1, :], reverse=(d == 1))
        h_refs[d][0] = h
        n = h.shape[0]
        carry[d:d + 1, :] = h[0:1, :] if d == 1 else h[n - 1:n, :]
    if emit_state:
        @pl.when(c == pl.num_programs(1) - 1)
        def _():
            hfin_ref[0] = carry[...]


def rglru(proj, gate_w16, conv_w, vecs, h0, seq_len, emit_state):
    bsz, t, _ = proj.shape
    tb = LRU_TB
    nt = t // tb
    assert t % tb == 0 and tb % seq_len == 0
    blk = lambda d: pl.BlockSpec((1, tb, WB), functools.partial(
        lambda bi, c, d, j: (bi, c if d == 0 else nt - 1 - c, j), d=d, j=COL_XB))
    oblk = lambda d: pl.BlockSpec((1, tb, WB), functools.partial(
        lambda bi, c, d: (bi, c if d == 0 else nt - 1 - c, 0), d=d))
    full = lambda a: pl.BlockSpec(a.shape, lambda bi, c: (0,) * a.ndim)
    sspec = pl.BlockSpec((1, 2, WB), lambda bi, c: (bi, 0, 0))
    out_specs = [oblk(0), oblk(1)]
    out_shape = [jax.ShapeDtypeStruct((bsz, t, WB), jnp.float32)] * 2
    if emit_state:
        out_specs.append(sspec)
        out_shape.append(jax.ShapeDtypeStruct((bsz, 2, WB), jnp.float32))
    return pl.pallas_call(
        functools.partial(_lru_kernel, seq_len=seq_len, emit_state=emit_state),
        grid=(bsz, nt),
        in_specs=[blk(0), blk(1), full(gate_w16), full(conv_w), full(vecs), sspec],
        out_specs=out_specs,
        out_shape=out_shape,
        scratch_shapes=[pltpu.VMEM((2, WB), jnp.float32)],
        compiler_params=pltpu.CompilerParams(
            dimension_semantics=("arbitrary", "arbitrary"), vmem_limit_bytes=VMEM_LIMIT),
        name="rglru",
    )(proj, proj, gate_w16, conv_w, vecs, h0)


def _block_diag(w):
    eye = jnp.eye(HB, dtype=w.dtype)
    return (w[:, :, None, :] * eye[:, None, :, None]).reshape(WB, WB)


def block(x, mod6, s0_bd, h0, emit_state, seq_len, final_norm, lp):
    bsz, t, dm = x.shape
    n = bsz * t
    n_mod = mod6.shape[0]
    assert (n // n_mod) % FFN_TILE == 0 and FFN_TILE % TOK_TILE == 0
    per_mod = lambda tile: (n // n_mod) // tile
    xf = x.reshape(n, dm)
    proj = in_proj(xf, mod6, lp['ln1_g'], lp['w_in16'], per_mod(TOK_TILE))
    proj3 = proj.reshape(bsz, t, -1)
    wkv = wkv7(proj3, lp['rw_wup_pad'], lp['rw_aup_pad'], lp['rw_vecs'], s0_bd, emit_state)
    lru = rglru(proj3, lp['lru_gate_w16'], lp['lru_conv_w'], lp['lru_vecs'], h0, min(seq_len, LRU_TB), emit_state)
    flat = lambda a: a.reshape(n, -1)
    xf = post_mix(xf, mod6, flat(wkv[0]), flat(wkv[1]), proj, flat(lru[0]), flat(lru[1]), lp['rw_aup_pad'],
                  lp['rw_gup_pad'], lp['post_vecs'], lp['w_out16'], per_mod(TOK_TILE))
    xf = ffn(xf, mod6, lp['ln2_g'], lp['w_up16'], lp['ffn_conv_w'], lp['ffn_conv_b'], lp['w_down16'], lp['lnf_g'],
             per_mod(FFN_TILE), seq_len, final_norm)
    return xf.reshape(bsz, t, dm), (wkv[2] if emit_state else None), (lru[2] if emit_state else None)


def kernel(x_prompt, x_sample, state_rwkv, state_lru, c, c_ctx, ln1_g, w_mod, b_mod, w_in, rw_w0, rw_w_up, rw_a0, rw_a_up, rw_g_up, rw_k_k, rw_k_a, rw_r_k, rw_lnx_g, rw_lnx_b, lru_conv_w, lru_conv_b, lru_ga_w, lru_ga_b, lru_gx_w, lru_gx_b, lru_lam, w_out, ln2_g, w_up, ffn_conv_w, ffn_conv_b, w_down, lnf_g):
    bf16 = jnp.bfloat16
    f32 = jnp.float32
    bp, tp, _ = x_prompt.shape
    bs = x_sample.shape[0]
    yp, ys = x_prompt, x_sample
    cc = jnp.concatenate([c, c_ctx[None], jnp.zeros((8 - bs - 1, D_MODEL), c.dtype)], axis=0)
    cc = jax.nn.silu(cc)
    new_rwkv, new_lru = [], []
    zero_h = jnp.zeros((bp, 2, WB), f32)
    zeros2 = jnp.zeros((2, WA), f32)
    perm = jnp.concatenate([jnp.arange(0, SPLITS[2]), jnp.arange(SPLITS[5], SPLITS[6] + WB),
                            jnp.arange(SPLITS[2], SPLITS[5])])
    for l in range(DEPTH):
        gate_w = jnp.stack([jnp.concatenate([_block_diag(lru_ga_w[l, d]), _block_diag(lru_gx_w[l, d])], axis=1)
                            for d in range(2)])
        lp = {'ln1_g': ln1_g[l], 'w_in16': w_in[l][:, perm].astype(bf16),
              'rw_wup_pad': jnp.pad(rw_w_up[l], ((0, 0), (0, GW - R_W), (0, 0))).astype(bf16),
              'rw_aup_pad': jnp.pad(rw_a_up[l], ((0, 0), (R_W, GW - R_W - R_A), (0, 0))).astype(bf16),
              'rw_gup_pad': jnp.pad(rw_g_up[l], ((R_W + R_A, 0), (0, 0))).astype(bf16),
              'rw_vecs': jnp.concatenate([rw_w0[l], rw_a0[l], rw_k_k[l][None], rw_k_a[l][None], zeros2], axis=0),
              'post_vecs': jnp.concatenate([rw_a0[l], rw_k_a[l][None], rw_r_k[l].reshape(1, WA), rw_lnx_g[l][None],
                                            rw_lnx_b[l][None], zeros2], axis=0),
              'lru_gate_w16': gate_w.astype(bf16), 'lru_conv_w': lru_conv_w[l],
              'lru_vecs': jnp.concatenate([lru_conv_b[l][None], lru_ga_b[l], lru_gx_b[l],
                                           jax.nn.softplus(-lru_lam[l]), jnp.zeros((1, WB), f32)], axis=0),
              'w_out16': w_out[l].astype(bf16), 'ln2_g': ln2_g[l], 'w_up16': w_up[l].astype(bf16),
              'ffn_conv_w': ffn_conv_w[l], 'ffn_conv_b': ffn_conv_b[l], 'w_down16': w_down[l].astype(bf16),
              'lnf_g': lnf_g}
        mod = mm(cc, w_mod[l], tm=8) + b_mod[l]
        final = l == DEPTH - 1
        yp, s_fin, h_fin = block(yp, mod[bs].reshape(1, 6, D_MODEL), None, zero_h, True, tp, final, lp)
        new_rwkv.append(jnp.moveaxis(_from_bd(s_fin), 0, 1))
        new_lru.append(h_fin)
        s0 = _to_bd(jnp.moveaxis(state_rwkv[:, l], 1, 0))
        ys, _, _ = block(ys, mod[:bs].reshape(bs, 6, D_MODEL), s0, state_lru[:, l], False, GRID_W, final, lp)
    return (yp, ys, jnp.stack(new_rwkv, axis=1), jnp.stack(new_lru, axis=1))
```

```python
import functools

import jax
import jax.numpy as jnp
from jax import lax
from jax.experimental import pallas as pl
from jax.experimental.pallas import tpu as pltpu

D_MODEL = 1024
DEPTH = 4
GRID_W = 64
WA = 512
HEAD_A = 64
HA = 8
WB = 512
HB = 8
BW = 64
R_W = 64
R_A = 64
R_G = 128
D_FF = 2816
CONV_B_LEFT = 2
CONV_F_LEFT = 1
LRU_C = 8.0
EPS = 1e-6
LNX_EPS = 64e-5
KK_EPS = 1e-12
DECAY_SCALE = 0.6065306597126334
SPLITS = (WA, 2 * WA, 3 * WA, 3 * WA + R_W, 3 * WA + R_W + R_A, 3 * WA + R_W + R_A + R_G,
          3 * WA + R_W + R_A + R_G + WB)

VMEM_LIMIT = 48 * 1024 * 1024
SUBLANES = 8

WKV_CHUNK = 64
WKV_GROUP = 4
GW = WKV_GROUP * HEAD_A
NGROUP = HA // WKV_GROUP


MOD_TN = 1536


def _mod_kernel(x_ref, w_ref, o_ref):
    o_ref[0] = jnp.dot(x_ref[...], w_ref[0], preferred_element_type=jnp.float32)


def mod_proj(x, w):
    m, k = x.shape
    n_layers, _, n = w.shape
    assert n % MOD_TN == 0
    return pl.pallas_call(
        _mod_kernel,
        grid=(n_layers, n // MOD_TN),
        in_specs=[pl.BlockSpec((m, k), lambda l, j: (0, 0)),
                  pl.BlockSpec((1, k, MOD_TN), lambda l, j: (l, 0, j))],
        out_specs=pl.BlockSpec((1, m, MOD_TN), lambda l, j: (l, 0, j)),
        out_shape=jax.ShapeDtypeStruct((n_layers, m, n), jnp.float32),
        compiler_params=pltpu.CompilerParams(
            dimension_semantics=("arbitrary", "arbitrary"), vmem_limit_bytes=VMEM_LIMIT),
        name="mod_proj",
    )(x, w)


def _bd(x, bdmask):
    return jnp.where(bdmask, jnp.concatenate([x] * WKV_GROUP, axis=0), 0.0).astype(jnp.bfloat16)


def _dot(a, b):
    return jnp.dot(a.astype(jnp.bfloat16), b.astype(jnp.bfloat16), preferred_element_type=jnp.float32)


def _dot_nt(a, b):
    return lax.dot_general(a.astype(jnp.bfloat16), b.astype(jnp.bfloat16), (((1,), (1,)), ((), ())),
                           preferred_element_type=jnp.float32)


def _dot_tn(a, b):
    return lax.dot_general(a.astype(jnp.bfloat16), b.astype(jnp.bfloat16), (((0,), (0,)), ((), ())),
                           preferred_element_type=jnp.float32)


def _split_bf16(x, n):
    parts = []
    for _ in range(n):
        p = x.astype(jnp.bfloat16)
        parts.append(p)
        x = x - p.astype(jnp.float32)
    return parts


def _wkv_masks(sgns):
    C = WKV_CHUNK
    row = lax.broadcasted_iota(jnp.int32, (C, C), 0)
    col = lax.broadcasted_iota(jnp.int32, (C, C), 1)
    t_w = lax.broadcasted_iota(jnp.int32, (C, GW), 0)
    s_w = lax.broadcasted_iota(jnp.int32, (C, GW), 1) & (C - 1)
    bdmask = (lax.broadcasted_iota(jnp.int32, (GW, GW), 0) // HEAD_A
              == lax.broadcasted_iota(jnp.int32, (GW, GW), 1) // HEAD_A)
    return dict(
        eye_w=t_w == s_w, bdmask=bdmask,
        tri={s: jnp.where((row - col) * s >= 0, 1.0, 0.0).astype(jnp.bfloat16) for s in set(sgns)},
        strict_w={s: (t_w - s_w) * s > 0 for s in set(sgns)},
        incl_w={s: (t_w - s_w) * s >= 0 for s in set(sgns)})


def _wkv_prepare(chains, masks, out):
    C = WKV_CHUNK
    n = len(chains)
    bf16 = jnp.bfloat16
    bdmask, eye_w = masks['bdmask'], masks['eye_w']
    ones_bd = jnp.where(bdmask, 1.0, 0.0).astype(bf16)
    sgns = [ch[0] for ch in chains]
    each = lambda f: [f(i) for i in range(n)]
    r, k, v, lw, a, k_k, k_a = (each(lambda i, j=j: chains[i][j]) for j in range(1, 8))

    kk = each(lambda i: k[i] * k_k[i])
    ssq = each(lambda i: jnp.dot(jnp.concatenate(_split_bf16(kk[i] * kk[i], 2), axis=0), ones_bd,
                                 preferred_element_type=jnp.float32))
    cum = each(lambda i: jnp.dot(masks['tri'][sgns[i]], jnp.concatenate(_split_bf16(lw[i], 3), axis=1),
                                 preferred_element_type=jnp.float32))
    yield
    ssq = each(lambda i: ssq[i][:C] + ssq[i][C:])
    cum = each(lambda i: cum[i][:, :GW] + cum[i][:, GW:2 * GW] + cum[i][:, 2 * GW:])
    kk = each(lambda i: kk[i] * lax.rsqrt(ssq[i] + KK_EPS))
    b = each(lambda i: kk[i] * a[i])
    kd = each(lambda i: k[i] * (1.0 + (a[i] - 1.0) * k_a[i]))
    tot = each(lambda i: cum[i][C - 1:C, :] if sgns[i] > 0 else cum[i][0:1, :])
    p_inv = each(lambda i: jnp.exp(-cum[i]))
    p_rem = each(lambda i: jnp.exp(tot[i] - cum[i]))
    ar = each(lambda i: jnp.concatenate([kk[i] * jnp.exp(cum[i] - lw[i]), r[i] * jnp.exp(cum[i])],
                                        axis=0).astype(bf16))
    bk_bd = each(lambda i: jnp.concatenate([_bd(b[i] * p_inv[i], bdmask), _bd(kd[i] * p_inv[i], bdmask)], axis=0))
    gmat = each(lambda i: _dot_nt(ar[i], bk_bd[i]))
    yield
    strict_w = each(lambda i: masks['strict_w'][sgns[i]])
    incl_w = each(lambda i: masks['incl_w'][sgns[i]])
    l_kappa = each(lambda i: jnp.where(strict_w[i], gmat[i][:C, GW:], 0.0))
    q_both = each(lambda i: jnp.concatenate([jnp.where(incl_w[i], gmat[i][C:, :GW], 0.0),
                                             jnp.where(incl_w[i], gmat[i][C:, GW:], 0.0)], axis=1).astype(bf16))
    m_pow = each(lambda i: jnp.where(strict_w[i], -gmat[i][:C, :GW], 0.0))
    t_inv = each(lambda i: jnp.where(eye_w, 1.0, 0.0) + m_pow[i])
    lkv = each(lambda i: _dot(l_kappa[i], _bd(v[i], bdmask)))
    m_pow = each(lambda i: _dot(m_pow[i], _bd(m_pow[i], bdmask)))
    yield
    for _ in range(C.bit_length() - 3):
        both = each(lambda i: _dot(jnp.concatenate([t_inv[i], m_pow[i]], axis=0), _bd(m_pow[i], bdmask)))
        yield
        t_inv = each(lambda i: t_inv[i] + both[i][:C])
        m_pow = each(lambda i: both[i][C:])
    t_inv = each(lambda i: t_inv[i] + _dot(t_inv[i], _bd(m_pow[i], bdmask)))
    yield
    for i in range(n):
        out.append((ar[i], t_inv[i].astype(bf16), lkv[i], q_both[i], v[i].astype(bf16),
                    jnp.concatenate([b[i] * p_rem[i], kd[i] * p_rem[i]], axis=0).astype(bf16), jnp.exp(tot[i])))


def _wkv_advance(prepared, states, masks, out):
    C = WKV_CHUNK
    n = len(prepared)
    bdmask = masks['bdmask']
    each = lambda f: [f(i) for i in range(n)]
    ar, t_inv, lkv, q_both, v, bkp, etot = (each(lambda i, j=j: prepared[i][j]) for j in range(7))
    ws = each(lambda i: _dot_nt(ar[i], states[i]))
    yield
    u = each(lambda i: -_dot(t_inv[i], _bd(ws[i][:C] + lkv[i], bdmask)))
    yield
    v_bd = each(lambda i: _bd(v[i].astype(jnp.float32), bdmask))
    y = each(lambda i: ws[i][C:] + _dot(q_both[i], jnp.concatenate([_bd(u[i], bdmask), v_bd[i]], axis=0)))
    ds = each(lambda i: _dot_tn(jnp.concatenate([u[i].astype(jnp.bfloat16), v[i]], axis=0), bkp[i]))
    yield
    for i in range(n):
        out.append((y[i], states[i] * etot[i] + jnp.where(bdmask, ds[i], 0.0)))


def _interleave(*gens):
    live = list(gens)
    while live:
        for g in list(live):
            try:
                next(g)
            except StopIteration:
                live.remove(g)


WKV_NB = 2


def _wkv_kernel(*refs, zero_init, emit_state):
    proj_refs = refs[:8]
    wup_ref, aup_ref, vec_ref = refs[8:11]
    pos = 11
    if not zero_init:
        s0_ref = refs[pos]
        pos += 1
    y_refs = refs[pos:pos + 2]
    pos += 2
    if emit_state:
        sfin_ref = refs[pos]
        pos += 1
    s_scr = refs[pos]
    c = pl.program_id(1)
    last = pl.num_programs(1) - 1

    @pl.when(c == 0)
    def _():
        if zero_init:
            s_scr[...] = jnp.zeros_like(s_scr)
        else:
            s_scr[...] = s0_ref[...]

    chain_ids = [(d, bi, g) for bi in range(WKV_NB) for d in range(2) for g in range(NGROUP)]
    raw = []
    for d, bi, g in chain_ids:
        r_ref, k_ref, v_ref, x_ref = proj_refs[4 * d:4 * d + 4]
        sl = slice(g * GW, (g + 1) * GW)
        x = x_ref[bi]
        lw = -DECAY_SCALE * jax.nn.sigmoid(vec_ref[d:d + 1, sl] + _dot(jnp.tanh(x), wup_ref[d, :, sl]))
        a = jax.nn.sigmoid(vec_ref[2 + d:3 + d, sl] + _dot(x, aup_ref[d, :, sl]))
        raw.append((1 - 2 * d, r_ref[bi, :, sl], k_ref[bi, :, sl], v_ref[bi, :, sl], lw, a, vec_ref[4:5, sl],
                    vec_ref[5:6, sl]))
    masks = _wkv_masks([1, -1])
    prepared, advanced = [], []
    _interleave(_wkv_prepare(raw, masks, prepared))
    _interleave(_wkv_advance(prepared, [s_scr[d, bi, g] for d, bi, g in chain_ids], masks, advanced))

    for (d, bi, g), (y, s_new) in zip(chain_ids, advanced):
        y_refs[d][bi, :, g * GW:(g + 1) * GW] = y
        s_scr[d, bi, g] = s_new
    if emit_state:
        @pl.when(c == last)
        def _():
            for (d, bi, g), (_, s_new) in zip(chain_ids, advanced):
                for h in range(WKV_GROUP):
                    blk = slice(h * HEAD_A, (h + 1) * HEAD_A)
                    sfin_ref[bi, d, g * WKV_GROUP + h] = s_new[blk, blk]


def wkv7(proj, wup_pad, aup_pad, vecs, s0_bd=None, emit_state=True):
    bsz, t, _ = proj.shape
    C = WKV_CHUNK
    nc = t // C
    assert t % C == 0 and bsz % WKV_NB == 0
    zero_init = s0_bd is None

    def chunk(d, c):
        return c if d == 0 else nc - 1 - c

    in_specs, args = [], []
    for d in range(2):
        for j in range(3):
            in_specs.append(pl.BlockSpec((WKV_NB, C, WA), functools.partial(lambda bi, c, d, j: (bi, chunk(d, c), j), d=d, j=j)))
        in_specs.append(pl.BlockSpec((WKV_NB, C, GW), functools.partial(lambda bi, c, d: (bi, chunk(d, c), COL_X), d=d)))
        args += [proj] * 4
    in_specs += [pl.BlockSpec((2, GW, WA), lambda bi, c: (0, 0, 0)), pl.BlockSpec((2, GW, WA), lambda bi, c: (0, 0, 0)),
                 pl.BlockSpec((8, WA), lambda bi, c: (0, 0))]
    args += [wup_pad, aup_pad, vecs]
    sspec = pl.BlockSpec((2, WKV_NB, NGROUP, GW, GW), lambda bi, c: (0, bi, 0, 0, 0))
    if not zero_init:
        in_specs.append(sspec)
        args.append(s0_bd)
    out_specs = [pl.BlockSpec((WKV_NB, C, WA), functools.partial(lambda bi, c, d: (bi, chunk(d, c), 0), d=d))
                 for d in range(2)]
    out_shape = [jax.ShapeDtypeStruct((bsz, t, WA), jnp.float32)] * 2
    if emit_state:
        out_specs.append(pl.BlockSpec((WKV_NB, 2, HA, HEAD_A, HEAD_A), lambda bi, c: (bi, 0, 0, 0, 0)))
        out_shape.append(jax.ShapeDtypeStruct((bsz, 2, HA, HEAD_A, HEAD_A), jnp.float32))
    return pl.pallas_call(
        functools.partial(_wkv_kernel, zero_init=zero_init, emit_state=emit_state),
        grid=(bsz // WKV_NB, nc),
        in_specs=in_specs,
        out_specs=out_specs,
        out_shape=out_shape,
        scratch_shapes=[pltpu.VMEM((2, WKV_NB, NGROUP, GW, GW), jnp.float32)],
        compiler_params=pltpu.CompilerParams(
            dimension_semantics=("arbitrary", "arbitrary"), vmem_limit_bytes=VMEM_LIMIT),
        name="wkv7",
    )(*args)


def _to_bd(s):
    n_dir, bsz = s.shape[:2]
    s = s.reshape(n_dir, bsz, NGROUP, WKV_GROUP, HEAD_A, HEAD_A)
    eye = jnp.eye(WKV_GROUP, dtype=s.dtype)
    out = s[:, :, :, :, :, None, :] * eye[None, None, None, :, None, :, None]
    return out.reshape(n_dir, bsz, NGROUP, GW, GW)


TOK_TILE = 512
FFN_TILE = 256
FFN_CHUNK = 256
LRU_TB = 256
COL_XB, COL_GB, COL_X = 3, 4, 10


def _rms(x):
    return x * lax.rsqrt(jnp.mean(x * x, axis=-1, keepdims=True) + EPS)


def _head_sums(x, ones_bd16):
    n = x.shape[0]
    s = jnp.dot(jnp.concatenate(_split_bf16(x, 2), axis=0), ones_bd16, preferred_element_type=jnp.float32)
    return s[:n] + s[n:]


def _in_kernel(x_ref, mod_ref, g_ref, w_ref, o_ref):
    h = _rms(x_ref[...]) * g_ref[...] * (1.0 + mod_ref[0, 1:2, :]) + mod_ref[0, 0:1, :]
    o_ref[...] = jnp.dot(h.astype(jnp.bfloat16), w_ref[...], preferred_element_type=jnp.float32)


def in_proj(x, mod6, ln_g, w16, tiles_per_mod):
    n, dm = x.shape
    p = w16.shape[1]
    return pl.pallas_call(
        _in_kernel,
        grid=(n // TOK_TILE,),
        in_specs=[pl.BlockSpec((TOK_TILE, dm), lambda i: (i, 0)),
                  pl.BlockSpec((1, 6, dm), lambda i: (i // tiles_per_mod, 0, 0)),
                  pl.BlockSpec((1, dm), lambda i: (0, 0)),
                  pl.BlockSpec((dm, p), lambda i: (0, 0))],
        out_specs=pl.BlockSpec((TOK_TILE, p), lambda i: (i, 0)),
        out_shape=jax.ShapeDtypeStruct((n, p), jnp.float32),
        compiler_params=pltpu.CompilerParams(dimension_semantics=("arbitrary",), vmem_limit_bytes=VMEM_LIMIT),
        name="in_proj",
    )(x, mod6, ln_g.reshape(1, dm), w16)


def _post_kernel(x_ref, mod_ref, y0_ref, y1_ref, r_ref, k_ref, v_ref, xx_ref, gb_ref, h0_ref, h1_ref,
                 aup_ref, gup_ref, vec_ref, wo_ref, o_ref):
    bf16 = jnp.bfloat16
    hr = lax.broadcasted_iota(jnp.int32, (WA, WA), 0) // HEAD_A
    hc = lax.broadcasted_iota(jnp.int32, (WA, WA), 1) // HEAD_A
    ones_bd16 = jnp.where(hr == hc, 1.0, 0.0).astype(bf16)
    xx = xx_ref[...]
    xx16 = xx.astype(bf16)
    a_sum = (jax.nn.sigmoid(vec_ref[0:1, :] + jnp.dot(xx16, aup_ref[0], preferred_element_type=jnp.float32))
             + jax.nn.sigmoid(vec_ref[1:2, :] + jnp.dot(xx16, aup_ref[1], preferred_element_type=jnp.float32)))
    gate = jnp.dot(jax.nn.sigmoid(xx).astype(bf16), gup_ref[...], preferred_element_type=jnp.float32)
    k = k_ref[...]
    k_both = k * (2.0 + (a_sum - 2.0) * vec_ref[2:3, :])
    bonus = _head_sums(r_ref[...] * k_both * vec_ref[3:4, :], ones_bd16) * v_ref[...]
    o = y0_ref[...] + y1_ref[...]
    mu = _head_sums(o, ones_bd16) * (1.0 / HEAD_A)
    oc = o - mu
    var = _head_sums(oc * oc, ones_bd16) * (1.0 / HEAD_A)
    ya = (oc * lax.rsqrt(var + LNX_EPS) * vec_ref[4:5, :] + vec_ref[5:6, :] + bonus) * gate
    yb = (h0_ref[...] + h1_ref[...]) * jax.nn.gelu(gb_ref[...])
    mix = (jnp.dot(ya.astype(bf16), wo_ref[:WA, :], preferred_element_type=jnp.float32)
           + jnp.dot(yb.astype(bf16), wo_ref[WA:, :], preferred_element_type=jnp.float32))
    o_ref[...] = x_ref[...] + mod_ref[0, 2:3, :] * mix


def post_mix(x, mod6, y0, y1, proj, h0, h1, aup_pad, gup_pad, vecs, wo16, tiles_per_mod):
    n, dm = x.shape
    tm = TOK_TILE
    tok = lambda w, j: pl.BlockSpec((tm, w), lambda i: (i, j))
    full = lambda a: pl.BlockSpec(a.shape, lambda i: (0,) * a.ndim)
    return pl.pallas_call(
        _post_kernel,
        grid=(n // tm,),
        in_specs=[tok(dm, 0), pl.BlockSpec((1, 6, dm), lambda i: (i // tiles_per_mod, 0, 0)),
                  tok(WA, 0), tok(WA, 0), tok(WA, 0), tok(WA, 1), tok(WA, 2), tok(GW, COL_X), tok(WA, COL_GB),
                  tok(WA, 0), tok(WA, 0), full(aup_pad), full(gup_pad), full(vecs), full(wo16)],
        out_specs=tok(dm, 0),
        out_shape=jax.ShapeDtypeStruct((n, dm), jnp.float32),
        compiler_params=pltpu.CompilerParams(dimension_semantics=("arbitrary",), vmem_limit_bytes=VMEM_LIMIT),
        name="post_mix",
    )(x, mod6, y0, y1, proj, proj, proj, proj, proj, h0, h1, aup_pad, gup_pad, vecs, wo16)


def _shifted(u, shift, seq_len):
    n = u.shape[0]
    pos = lax.broadcasted_iota(jnp.int32, u.shape, 0) & (seq_len - 1)
    rolled = pltpu.roll(u, (-shift) % n, 0)
    ok = (pos + shift >= 0) & (pos + shift < seq_len)
    return jnp.where(ok, rolled, 0.0)


GELU_C0 = 0.7978845608028654
GELU_C1 = GELU_C0 * 0.044715


def _gelu_tanh(x):
    return (0.5 * x) * (1.0 + jnp.tanh(x * (GELU_C0 + GELU_C1 * (x * x))))


def _ffn_kernel(x_ref, mod_ref, g_ref, wu_ref, cw_ref, cb_ref, wd_ref, gf_ref, o_ref, hid_scr, *, seq_len, final_norm):
    x = x_ref[...]
    h16 = (_rms(x) * g_ref[...] * (1.0 + mod_ref[0, 4:5, :]) + mod_ref[0, 3:4, :]).astype(jnp.bfloat16)

    def conv(lo):
        cols = slice(lo, lo + FFN_CHUNK)
        u = jnp.dot(h16, wu_ref[:, cols], preferred_element_type=jnp.float32)
        return (cb_ref[:, cols] + _shifted(u, -1, seq_len) * cw_ref[0:1, cols] + u * cw_ref[1:2, cols]
                + _shifted(u, 1, seq_len) * cw_ref[2:3, cols])

    for j in range(D_FF // FFN_CHUNK):
        lo = j * FFN_CHUNK
        hid_scr[:, lo:lo + FFN_CHUNK] = (_gelu_tanh(conv(lo)) * conv(D_FF + lo)).astype(jnp.bfloat16)
    out = x + mod_ref[0, 5:6, :] * jnp.dot(hid_scr[...], wd_ref[...], preferred_element_type=jnp.float32)
    if final_norm:
        out = _rms(out) * gf_ref[...]
    o_ref[...] = out


def ffn(x, mod6, ln_g, wup16, conv_w, conv_b, wdown16, lnf_g, tiles_per_mod, seq_len, final_norm):
    n, dm = x.shape
    tm = FFN_TILE
    assert D_FF % FFN_CHUNK == 0 and tm % seq_len == 0 and n % tm == 0
    resident = lambda a: pl.BlockSpec(a.shape, lambda i: (0,) * a.ndim, pipeline_mode=pl.Buffered(1))
    cb = conv_b.reshape(1, 2 * D_FF)
    return pl.pallas_call(
        functools.partial(_ffn_kernel, seq_len=seq_len, final_norm=final_norm),
        grid=(n // tm,),
        in_specs=[pl.BlockSpec((tm, dm), lambda i: (i, 0)),
                  pl.BlockSpec((1, 6, dm), lambda i: (i // tiles_per_mod, 0, 0)),
                  pl.BlockSpec((1, dm), lambda i: (0, 0)),
                  resident(wup16), resident(conv_w), resident(cb), resident(wdown16),
                  pl.BlockSpec((1, dm), lambda i: (0, 0))],
        out_specs=pl.BlockSpec((tm, dm), lambda i: (i, 0)),
        out_shape=jax.ShapeDtypeStruct((n, dm), jnp.float32),
        scratch_shapes=[pltpu.VMEM((tm, D_FF), jnp.bfloat16)],
        compiler_params=pltpu.CompilerParams(dimension_semantics=("arbitrary",), vmem_limit_bytes=VMEM_LIMIT),
        name="ffn",
    )(x, mod6, ln_g.reshape(1, dm), wup16, conv_w, cb, wdown16, lnf_g.reshape(1, dm))


def _lru_scan_block(a, u, carry, reverse):
    n = a.shape[0]
    row = lax.broadcasted_iota(jnp.int32, a.shape, 0)
    s = 1
    while s < n:
        if s < SUBLANES:
            ok = (row < n - s) if reverse else (row >= s)
            shift = (n - s) if reverse else s
            a_prev = jnp.where(ok, pltpu.roll(a, shift, 0), 1.0)
            u_prev = jnp.where(ok, pltpu.roll(u, shift, 0), 0.0)
            u = a * u_prev + u
            a = a * a_prev
        elif reverse:
            u = jnp.concatenate([a[:n - s] * u[s:] + u[:n - s], u[n - s:]], axis=0)
            a = jnp.concatenate([a[:n - s] * a[s:], a[n - s:]], axis=0)
        else:
            u = jnp.concatenate([u[:s], a[s:] * u[:n - s] + u[s:]], axis=0)
            a = jnp.concatenate([a[:s], a[s:] * a[:n - s]], axis=0)
        s *= 2
    return a * carry + u


def _lru_kernel(*refs, seq_len, emit_state):
    xb_refs = refs[0:2]
    gw_ref, cw_ref, vec_ref, h0_ref = refs[2:6]
    h_refs = refs[6:8]
    pos = 8
    if emit_state:
        hfin_ref = refs[pos]
        pos += 1
    carry = refs[pos]
    c = pl.program_id(1)

    @pl.when(c == 0)
    def _():
        carry[...] = h0_ref[0]

    for d in range(2):
        xb = xb_refs[d][0]
        xc = vec_ref[0:1, :]
        for j in range(4):
            xc = xc + _shifted(xb, j - CONV_B_LEFT, seq_len) * cw_ref[j:j + 1, :]
        gates = jnp.dot(xc.astype(jnp.bfloat16), gw_ref[d], preferred_element_type=jnp.float32)
        rg = jax.nn.sigmoid(gates[:, :WB] + vec_ref[1 + d:2 + d, :])
        ig = jax.nn.sigmoid(gates[:, WB:] + vec_ref[3 + d:4 + d, :])
        log_a = -LRU_C * rg * vec_ref[5 + d:6 + d, :]
        a = jnp.exp(log_a)
        u = jnp.sqrt(-jnp.tanh(log_a) * (a * a + 1.0)) * (ig * xc)
        h = _lru_scan_block(a, u, carry[d:d + 1, :], reverse=(d == 1))
        h_refs[d][0] = h
        n = h.shape[0]
        carry[d:d + 1, :] = h[0:1, :] if d == 1 else h[n - 1:n, :]
    if emit_state:
        @pl.when(c == pl.num_programs(1) - 1)
        def _():
            hfin_ref[0] = carry[...]


def rglru(proj, gate_w16, conv_w, vecs, h0, seq_len, emit_state):
    bsz, t, _ = proj.shape
    tb = LRU_TB
    nt = t // tb
    assert t % tb == 0 and tb % seq_len == 0
    blk = lambda d: pl.BlockSpec((1, tb, WB), functools.partial(
        lambda bi, c, d, j: (bi, c if d == 0 else nt - 1 - c, j), d=d, j=COL_XB))
    oblk = lambda d: pl.BlockSpec((1, tb, WB), functools.partial(
        lambda bi, c, d: (bi, c if d == 0 else nt - 1 - c, 0), d=d))
    full = lambda a: pl.BlockSpec(a.shape, lambda bi, c: (0,) * a.ndim)
    sspec = pl.BlockSpec((1, 2, WB), lambda bi, c: (bi, 0, 0))
    out_specs = [oblk(0), oblk(1)]
    out_shape = [jax.ShapeDtypeStruct((bsz, t, WB), jnp.float32)] * 2
    if emit_state:
        out_specs.append(sspec)
        out_shape.append(jax.ShapeDtypeStruct((bsz, 2, WB), jnp.float32))
    return pl.pallas_call(
        functools.partial(_lru_kernel, seq_len=seq_len, emit_state=emit_state),
        grid=(bsz, nt),
        in_specs=[blk(0), blk(1), full(gate_w16), full(conv_w), full(vecs), sspec],
        out_specs=out_specs,
        out_shape=out_shape,
        scratch_shapes=[pltpu.VMEM((2, WB), jnp.float32)],
        compiler_params=pltpu.CompilerParams(
            dimension_semantics=("arbitrary", "arbitrary"), vmem_limit_bytes=VMEM_LIMIT),
        name="rglru",
    )(proj, proj, gate_w16, conv_w, vecs, h0)


def _block_diag(w):
    eye = jnp.eye(HB, dtype=w.dtype)
    return (w[:, :, None, :] * eye[:, None, :, None]).reshape(WB, WB)


def block(x, mod6, s0_bd, h0, emit_state, seq_len, final_norm, lp):
    bsz, t, dm = x.shape
    n = bsz * t
    n_mod = mod6.shape[0]
    assert (n // n_mod) % TOK_TILE == 0 and TOK_TILE % FFN_TILE == 0
    per_mod = lambda tile: (n // n_mod) // tile
    xf = x.reshape(n, dm)
    proj = in_proj(xf, mod6, lp['ln1_g'], lp['w_in16'], per_mod(TOK_TILE))
    proj3 = proj.reshape(bsz, t, -1)
    wkv = wkv7(proj3, lp['rw_wup_pad'], lp['rw_aup_pad'], lp['rw_vecs'], s0_bd, emit_state)
    lru = rglru(proj3, lp['lru_gate_w16'], lp['lru_conv_w'], lp['lru_vecs'], h0, min(seq_len, LRU_TB), emit_state)
    flat = lambda a: a.reshape(n, -1)
    xf = post_mix(xf, mod6, flat(wkv[0]), flat(wkv[1]), proj, flat(lru[0]), flat(lru[1]), lp['rw_aup_pad'],
                  lp['rw_gup_pad'], lp['post_vecs'], lp['w_out16'], per_mod(TOK_TILE))
    xf = ffn(xf, mod6, lp['ln2_g'], lp['w_up16'], lp['ffn_conv_w'], lp['ffn_conv_b'], lp['w_down16'], lp['lnf_g'],
             per_mod(FFN_TILE), seq_len, final_norm)
    return xf.reshape(bsz, t, dm), (wkv[2] if emit_state else None), (lru[2] if emit_state else None)


def kernel(x_prompt, x_sample, state_rwkv, state_lru, c, c_ctx, ln1_g, w_mod, b_mod, w_in, rw_w0, rw_w_up, rw_a0, rw_a_up, rw_g_up, rw_k_k, rw_k_a, rw_r_k, rw_lnx_g, rw_lnx_b, lru_conv_w, lru_conv_b, lru_ga_w, lru_ga_b, lru_gx_w, lru_gx_b, lru_lam, w_out, ln2_g, w_up, ffn_conv_w, ffn_conv_b, w_down, lnf_g):
    bf16 = jnp.bfloat16
    f32 = jnp.float32
    bp, tp, _ = x_prompt.shape
    bs = x_sample.shape[0]
    yp, ys = x_prompt, x_sample
    cc = jnp.concatenate([c, c_ctx[None], jnp.zeros((8 - bs - 1, D_MODEL), c.dtype)], axis=0)
    cc = jax.nn.silu(cc)
    mod_all = mod_proj(cc, w_mod) + b_mod[:, None, :]
    new_rwkv, new_lru = [], []
    zero_h = jnp.zeros((bp, 2, WB), f32)
    zeros2 = jnp.zeros((2, WA), f32)
    perm = jnp.concatenate([jnp.arange(0, SPLITS[2]), jnp.arange(SPLITS[5], SPLITS[6] + WB),
                            jnp.arange(SPLITS[2], SPLITS[5])])
    for l in range(DEPTH):
        gate_w = jnp.stack([jnp.concatenate([_block_diag(lru_ga_w[l, d]), _block_diag(lru_gx_w[l, d])], axis=1)
                            for d in range(2)])
        lp = {'ln1_g': ln1_g[l], 'w_in16': w_in[l][:, perm].astype(bf16),
              'rw_wup_pad': jnp.pad(rw_w_up[l], ((0, 0), (0, GW - R_W), (0, 0))).astype(bf16),
              'rw_aup_pad': jnp.pad(rw_a_up[l], ((0, 0), (R_W, GW - R_W - R_A), (0, 0))).astype(bf16),
              'rw_gup_pad': jnp.pad(rw_g_up[l], ((R_W + R_A, 0), (0, 0))).astype(bf16),
              'rw_vecs': jnp.concatenate([rw_w0[l], rw_a0[l], rw_k_k[l][None], rw_k_a[l][None], zeros2], axis=0),
              'post_vecs': jnp.concatenate([rw_a0[l], rw_k_a[l][None], rw_r_k[l].reshape(1, WA), rw_lnx_g[l][None],
                                            rw_lnx_b[l][None], zeros2], axis=0),
              'lru_gate_w16': gate_w.astype(bf16), 'lru_conv_w': lru_conv_w[l],
              'lru_vecs': jnp.concatenate([lru_conv_b[l][None], lru_ga_b[l], lru_gx_b[l],
                                           jax.nn.softplus(-lru_lam[l]), jnp.zeros((1, WB), f32)], axis=0),
              'w_out16': w_out[l].astype(bf16), 'ln2_g': ln2_g[l], 'w_up16': w_up[l].astype(bf16),
              'ffn_conv_w': ffn_conv_w[l], 'ffn_conv_b': ffn_conv_b[l], 'w_down16': w_down[l].astype(bf16),
              'lnf_g': lnf_g}
        mod = mod_all[l]
        final = l == DEPTH - 1
        yp, s_fin, h_fin = block(yp, mod[bs].reshape(1, 6, D_MODEL), None, zero_h, True, tp, final, lp)
        new_rwkv.append(s_fin)
        new_lru.append(h_fin)
        s0 = _to_bd(jnp.moveaxis(state_rwkv[:, l], 1, 0))
        ys, _, _ = block(ys, mod[:bs].reshape(bs, 6, D_MODEL), s0, state_lru[:, l], False, GRID_W, final, lp)
    return (yp, ys, jnp.stack(new_rwkv, axis=1), jnp.stack(new_lru, axis=1))
```

```python
import functools

import jax
import jax.numpy as jnp
from jax import lax
from jax.experimental import pallas as pl
from jax.experimental.pallas import tpu as pltpu

D_MODEL = 1024
DEPTH = 4
GRID_W = 64
WA = 512
HEAD_A = 64
HA = 8
WB = 512
HB = 8
BW = 64
R_W = 64
R_A = 64
R_G = 128
D_FF = 2816
CONV_B_LEFT = 2
CONV_F_LEFT = 1
LRU_C = 8.0
EPS = 1e-6
LNX_EPS = 64e-5
KK_EPS = 1e-12
DECAY_SCALE = 0.6065306597126334
SPLITS = (WA, 2 * WA, 3 * WA, 3 * WA + R_W, 3 * WA + R_W + R_A, 3 * WA + R_W + R_A + R_G,
          3 * WA + R_W + R_A + R_G + WB)

VMEM_LIMIT = 48 * 1024 * 1024
SUBLANES = 8

WKV_CHUNK = 64
WKV_GROUP = 4
GW = WKV_GROUP * HEAD_A
NGROUP = HA // WKV_GROUP


MOD_TN = 1536


def _mod_kernel(x_ref, w_ref, o_ref):
    o_ref[0] = jnp.dot(x_ref[...], w_ref[0], preferred_element_type=jnp.float32)


def mod_proj(x, w):
    m, k = x.shape
    n_layers, _, n = w.shape
    assert n % MOD_TN == 0
    return pl.pallas_call(
        _mod_kernel,
        grid=(n_layers, n // MOD_TN),
        in_specs=[pl.BlockSpec((m, k), lambda l, j: (0, 0)),
                  pl.BlockSpec((1, k, MOD_TN), lambda l, j: (l, 0, j))],
        out_specs=pl.BlockSpec((1, m, MOD_TN), lambda l, j: (l, 0, j)),
        out_shape=jax.ShapeDtypeStruct((n_layers, m, n), jnp.float32),
        compiler_params=pltpu.CompilerParams(
            dimension_semantics=("arbitrary", "arbitrary"), vmem_limit_bytes=VMEM_LIMIT),
        name="mod_proj",
    )(x, w)


def _bd(x, bdmask):
    return jnp.where(bdmask, jnp.concatenate([x] * WKV_GROUP, axis=0), 0.0).astype(jnp.bfloat16)


def _dot(a, b):
    return jnp.dot(a.astype(jnp.bfloat16), b.astype(jnp.bfloat16), preferred_element_type=jnp.float32)


def _dot_nt(a, b):
    return lax.dot_general(a.astype(jnp.bfloat16), b.astype(jnp.bfloat16), (((1,), (1,)), ((), ())),
                           preferred_element_type=jnp.float32)


def _dot_tn(a, b):
    return lax.dot_general(a.astype(jnp.bfloat16), b.astype(jnp.bfloat16), (((0,), (0,)), ((), ())),
                           preferred_element_type=jnp.float32)


def _split_bf16(x, n):
    parts = []
    for _ in range(n):
        p = x.astype(jnp.bfloat16)
        parts.append(p)
        x = x - p.astype(jnp.float32)
    return parts


def _wkv_masks(sgns):
    C = WKV_CHUNK
    row = lax.broadcasted_iota(jnp.int32, (C, C), 0)
    col = lax.broadcasted_iota(jnp.int32, (C, C), 1)
    t_w = lax.broadcasted_iota(jnp.int32, (C, GW), 0)
    s_w = lax.broadcasted_iota(jnp.int32, (C, GW), 1) & (C - 1)
    bdmask = (lax.broadcasted_iota(jnp.int32, (GW, GW), 0) // HEAD_A
              == lax.broadcasted_iota(jnp.int32, (GW, GW), 1) // HEAD_A)
    return dict(
        eye_w=t_w == s_w, bdmask=bdmask,
        tri={s: jnp.where((row - col) * s >= 0, 1.0, 0.0).astype(jnp.bfloat16) for s in set(sgns)},
        strict_w={s: (t_w - s_w) * s > 0 for s in set(sgns)},
        incl_w={s: (t_w - s_w) * s >= 0 for s in set(sgns)})


def _wkv_prepare(chains, masks, out):
    C = WKV_CHUNK
    n = len(chains)
    bf16 = jnp.bfloat16
    bdmask, eye_w = masks['bdmask'], masks['eye_w']
    ones_bd = jnp.where(bdmask, 1.0, 0.0).astype(bf16)
    sgns = [ch[0] for ch in chains]
    each = lambda f: [f(i) for i in range(n)]
    r, k, v, lw, a, k_k, k_a = (each(lambda i, j=j: chains[i][j]) for j in range(1, 8))

    kk = each(lambda i: k[i] * k_k[i])
    ssq = each(lambda i: jnp.dot(jnp.concatenate(_split_bf16(kk[i] * kk[i], 2), axis=0), ones_bd,
                                 preferred_element_type=jnp.float32))
    cum = each(lambda i: jnp.dot(masks['tri'][sgns[i]], jnp.concatenate(_split_bf16(lw[i], 3), axis=1),
                                 preferred_element_type=jnp.float32))
    yield
    ssq = each(lambda i: ssq[i][:C] + ssq[i][C:])
    cum = each(lambda i: cum[i][:, :GW] + cum[i][:, GW:2 * GW] + cum[i][:, 2 * GW:])
    kk = each(lambda i: kk[i] * lax.rsqrt(ssq[i] + KK_EPS))
    b = each(lambda i: kk[i] * a[i])
    kd = each(lambda i: k[i] * (1.0 + (a[i] - 1.0) * k_a[i]))
    tot = each(lambda i: cum[i][C - 1:C, :] if sgns[i] > 0 else cum[i][0:1, :])
    p_inv = each(lambda i: jnp.exp(-cum[i]))
    p_rem = each(lambda i: jnp.exp(tot[i] - cum[i]))
    ar = each(lambda i: jnp.concatenate([kk[i] * jnp.exp(cum[i] - lw[i]), r[i] * jnp.exp(cum[i])],
                                        axis=0).astype(bf16))
    bk_bd = each(lambda i: jnp.concatenate([_bd(b[i] * p_inv[i], bdmask), _bd(kd[i] * p_inv[i], bdmask)], axis=0))
    gmat = each(lambda i: _dot_nt(ar[i], bk_bd[i]))
    yield
    strict_w = each(lambda i: masks['strict_w'][sgns[i]])
    incl_w = each(lambda i: masks['incl_w'][sgns[i]])
    l_kappa = each(lambda i: jnp.where(strict_w[i], gmat[i][:C, GW:], 0.0))
    q_both = each(lambda i: jnp.concatenate([jnp.where(incl_w[i], gmat[i][C:, :GW], 0.0),
                                             jnp.where(incl_w[i], gmat[i][C:, GW:], 0.0)], axis=1).astype(bf16))
    m_pow = each(lambda i: jnp.where(strict_w[i], -gmat[i][:C, :GW], 0.0))
    t_inv = each(lambda i: jnp.where(eye_w, 1.0, 0.0) + m_pow[i])
    lkv = each(lambda i: _dot(l_kappa[i], _bd(v[i], bdmask)))
    m_pow = each(lambda i: _dot(m_pow[i], _bd(m_pow[i], bdmask)))
    yield
    for _ in range(C.bit_length() - 3):
        both = each(lambda i: _dot(jnp.concatenate([t_inv[i], m_pow[i]], axis=0), _bd(m_pow[i], bdmask)))
        yield
        t_inv = each(lambda i: t_inv[i] + both[i][:C])
        m_pow = each(lambda i: both[i][C:])
    t_inv = each(lambda i: t_inv[i] + _dot(t_inv[i], _bd(m_pow[i], bdmask)))
    yield
    for i in range(n):
        out.append((ar[i], t_inv[i].astype(bf16), lkv[i], q_both[i], v[i].astype(bf16),
                    jnp.concatenate([b[i] * p_rem[i], kd[i] * p_rem[i]], axis=0).astype(bf16), jnp.exp(tot[i])))


def _wkv_advance(prepared, states, masks, out):
    C = WKV_CHUNK
    n = len(prepared)
    bdmask = masks['bdmask']
    each = lambda f: [f(i) for i in range(n)]
    ar, t_inv, lkv, q_both, v, bkp, etot = (each(lambda i, j=j: prepared[i][j]) for j in range(7))
    ws = each(lambda i: _dot_nt(ar[i], states[i]))
    yield
    u = each(lambda i: -_dot(t_inv[i], _bd(ws[i][:C] + lkv[i], bdmask)))
    yield
    v_bd = each(lambda i: _bd(v[i].astype(jnp.float32), bdmask))
    y = each(lambda i: ws[i][C:] + _dot(q_both[i], jnp.concatenate([_bd(u[i], bdmask), v_bd[i]], axis=0)))
    ds = each(lambda i: _dot_tn(jnp.concatenate([u[i].astype(jnp.bfloat16), v[i]], axis=0), bkp[i]))
    yield
    for i in range(n):
        out.append((y[i], states[i] * etot[i] + jnp.where(bdmask, ds[i], 0.0)))


def _interleave(*gens):
    live = list(gens)
    while live:
        for g in list(live):
            try:
                next(g)
            except StopIteration:
                live.remove(g)


WKV_NB = 4


def _wkv_kernel(*refs, zero_init, emit_state):
    proj_refs = refs[:8]
    wup_ref, aup_ref, vec_ref = refs[8:11]
    pos = 11
    if not zero_init:
        s0_ref = refs[pos]
        pos += 1
    y_refs = refs[pos:pos + 2]
    pos += 2
    if emit_state:
        sfin_ref = refs[pos]
        pos += 1
    s_scr = refs[pos]
    c = pl.program_id(1)
    last = pl.num_programs(1) - 1

    @pl.when(c == 0)
    def _():
        if zero_init:
            s_scr[...] = jnp.zeros_like(s_scr)
        else:
            s_scr[...] = s0_ref[...]

    chain_ids = [(d, bi, g) for bi in range(WKV_NB) for d in range(2) for g in range(NGROUP)]
    raw = []
    for d, bi, g in chain_ids:
        r_ref, k_ref, v_ref, x_ref = proj_refs[4 * d:4 * d + 4]
        sl = slice(g * GW, (g + 1) * GW)
        x = x_ref[bi]
        lw = -DECAY_SCALE * jax.nn.sigmoid(vec_ref[d:d + 1, sl] + _dot(jnp.tanh(x), wup_ref[d, :, sl]))
        a = jax.nn.sigmoid(vec_ref[2 + d:3 + d, sl] + _dot(x, aup_ref[d, :, sl]))
        raw.append((1 - 2 * d, r_ref[bi, :, sl], k_ref[bi, :, sl], v_ref[bi, :, sl], lw, a, vec_ref[4:5, sl],
                    vec_ref[5:6, sl]))
    masks = _wkv_masks([1, -1])
    prepared, advanced = [], []
    _interleave(_wkv_prepare(raw, masks, prepared))
    _interleave(_wkv_advance(prepared, [s_scr[d, bi, g] for d, bi, g in chain_ids], masks, advanced))

    for (d, bi, g), (y, s_new) in zip(chain_ids, advanced):
        y_refs[d][bi, :, g * GW:(g + 1) * GW] = y
        s_scr[d, bi, g] = s_new
    if emit_state:
        @pl.when(c == last)
        def _():
            for (d, bi, g), (_, s_new) in zip(chain_ids, advanced):
                for h in range(WKV_GROUP):
                    blk = slice(h * HEAD_A, (h + 1) * HEAD_A)
                    sfin_ref[bi, d, g * WKV_GROUP + h] = s_new[blk, blk]


def wkv7(proj, wup_pad, aup_pad, vecs, s0_bd=None, emit_state=True):
    bsz, t, _ = proj.shape
    C = WKV_CHUNK
    nc = t // C
    assert t % C == 0 and bsz % WKV_NB == 0
    zero_init = s0_bd is None

    def chunk(d, c):
        return c if d == 0 else nc - 1 - c

    in_specs, args = [], []
    for d in range(2):
        for j in range(3):
            in_specs.append(pl.BlockSpec((WKV_NB, C, WA), functools.partial(lambda bi, c, d, j: (bi, chunk(d, c), j), d=d, j=j)))
        in_specs.append(pl.BlockSpec((WKV_NB, C, GW), functools.partial(lambda bi, c, d: (bi, chunk(d, c), COL_X), d=d)))
        args += [proj] * 4
    in_specs += [pl.BlockSpec((2, GW, WA), lambda bi, c: (0, 0, 0)), pl.BlockSpec((2, GW, WA), lambda bi, c: (0, 0, 0)),
                 pl.BlockSpec((8, WA), lambda bi, c: (0, 0))]
    args += [wup_pad, aup_pad, vecs]
    sspec = pl.BlockSpec((2, WKV_NB, NGROUP, GW, GW), lambda bi, c: (0, bi, 0, 0, 0))
    if not zero_init:
        in_specs.append(sspec)
        args.append(s0_bd)
    out_specs = [pl.BlockSpec((WKV_NB, C, WA), functools.partial(lambda bi, c, d: (bi, chunk(d, c), 0), d=d))
                 for d in range(2)]
    out_shape = [jax.ShapeDtypeStruct((bsz, t, WA), jnp.float32)] * 2
    if emit_state:
        out_specs.append(pl.BlockSpec((WKV_NB, 2, HA, HEAD_A, HEAD_A), lambda bi, c: (bi, 0, 0, 0, 0)))
        out_shape.append(jax.ShapeDtypeStruct((bsz, 2, HA, HEAD_A, HEAD_A), jnp.float32))
    return pl.pallas_call(
        functools.partial(_wkv_kernel, zero_init=zero_init, emit_state=emit_state),
        grid=(bsz // WKV_NB, nc),
        in_specs=in_specs,
        out_specs=out_specs,
        out_shape=out_shape,
        scratch_shapes=[pltpu.VMEM((2, WKV_NB, NGROUP, GW, GW), jnp.float32)],
        compiler_params=pltpu.CompilerParams(
            dimension_semantics=("arbitrary", "arbitrary"), vmem_limit_bytes=VMEM_LIMIT),
        name="wkv7",
    )(*args)


def _to_bd(s):
    n_dir, bsz = s.shape[:2]
    s = s.reshape(n_dir, bsz, NGROUP, WKV_GROUP, HEAD_A, HEAD_A)
    eye = jnp.eye(WKV_GROUP, dtype=s.dtype)
    out = s[:, :, :, :, :, None, :] * eye[None, None, None, :, None, :, None]
    return out.reshape(n_dir, bsz, NGROUP, GW, GW)


TOK_TILE = 512
FFN_TILE = 256
FFN_CHUNK = 256
LRU_TB = 256
COL_XB, COL_GB, COL_X = 3, 4, 10


def _rms(x):
    return x * lax.rsqrt(jnp.mean(x * x, axis=-1, keepdims=True) + EPS)


def _head_sums(x, ones_bd16):
    n = x.shape[0]
    s = jnp.dot(jnp.concatenate(_split_bf16(x, 2), axis=0), ones_bd16, preferred_element_type=jnp.float32)
    return s[:n] + s[n:]


def _in_kernel(x_ref, mod_ref, g_ref, w_ref, o_ref):
    h = _rms(x_ref[...]) * g_ref[...] * (1.0 + mod_ref[0, 1:2, :]) + mod_ref[0, 0:1, :]
    o_ref[...] = jnp.dot(h.astype(jnp.bfloat16), w_ref[...], preferred_element_type=jnp.float32)


def in_proj(x, mod6, ln_g, w16, tiles_per_mod):
    n, dm = x.shape
    p = w16.shape[1]
    return pl.pallas_call(
        _in_kernel,
        grid=(n // TOK_TILE,),
        in_specs=[pl.BlockSpec((TOK_TILE, dm), lambda i: (i, 0)),
                  pl.BlockSpec((1, 6, dm), lambda i: (i // tiles_per_mod, 0, 0)),
                  pl.BlockSpec((1, dm), lambda i: (0, 0)),
                  pl.BlockSpec((dm, p), lambda i: (0, 0))],
        out_specs=pl.BlockSpec((TOK_TILE, p), lambda i: (i, 0)),
        out_shape=jax.ShapeDtypeStruct((n, p), jnp.float32),
        compiler_params=pltpu.CompilerParams(dimension_semantics=("arbitrary",), vmem_limit_bytes=VMEM_LIMIT),
        name="in_proj",
    )(x, mod6, ln_g.reshape(1, dm), w16)


def _post_kernel(x_ref, mod_ref, y0_ref, y1_ref, r_ref, k_ref, v_ref, xx_ref, gb_ref, h0_ref, h1_ref,
                 aup_ref, gup_ref, vec_ref, wo_ref, o_ref):
    bf16 = jnp.bfloat16
    hr = lax.broadcasted_iota(jnp.int32, (WA, WA), 0) // HEAD_A
    hc = lax.broadcasted_iota(jnp.int32, (WA, WA), 1) // HEAD_A
    ones_bd16 = jnp.where(hr == hc, 1.0, 0.0).astype(bf16)
    xx = xx_ref[...]
    xx16 = xx.astype(bf16)
    a_sum = (jax.nn.sigmoid(vec_ref[0:1, :] + jnp.dot(xx16, aup_ref[0], preferred_element_type=jnp.float32))
             + jax.nn.sigmoid(vec_ref[1:2, :] + jnp.dot(xx16, aup_ref[1], preferred_element_type=jnp.float32)))
    gate = jnp.dot(jax.nn.sigmoid(xx).astype(bf16), gup_ref[...], preferred_element_type=jnp.float32)
    k = k_ref[...]
    k_both = k * (2.0 + (a_sum - 2.0) * vec_ref[2:3, :])
    bonus = _head_sums(r_ref[...] * k_both * vec_ref[3:4, :], ones_bd16) * v_ref[...]
    o = y0_ref[...] + y1_ref[...]
    mu = _head_sums(o, ones_bd16) * (1.0 / HEAD_A)
    oc = o - mu
    var = _head_sums(oc * oc, ones_bd16) * (1.0 / HEAD_A)
    ya = (oc * lax.rsqrt(var + LNX_EPS) * vec_ref[4:5, :] + vec_ref[5:6, :] + bonus) * gate
    yb = (h0_ref[...] + h1_ref[...]) * jax.nn.gelu(gb_ref[...])
    mix = (jnp.dot(ya.astype(bf16), wo_ref[:WA, :], preferred_element_type=jnp.float32)
           + jnp.dot(yb.astype(bf16), wo_ref[WA:, :], preferred_element_type=jnp.float32))
    o_ref[...] = x_ref[...] + mod_ref[0, 2:3, :] * mix


def post_mix(x, mod6, y0, y1, proj, h0, h1, aup_pad, gup_pad, vecs, wo16, tiles_per_mod):
    n, dm = x.shape
    tm = TOK_TILE
    tok = lambda w, j: pl.BlockSpec((tm, w), lambda i: (i, j))
    full = lambda a: pl.BlockSpec(a.shape, lambda i: (0,) * a.ndim)
    return pl.pallas_call(
        _post_kernel,
        grid=(n // tm,),
        in_specs=[tok(dm, 0), pl.BlockSpec((1, 6, dm), lambda i: (i // tiles_per_mod, 0, 0)),
                  tok(WA, 0), tok(WA, 0), tok(WA, 0), tok(WA, 1), tok(WA, 2), tok(GW, COL_X), tok(WA, COL_GB),
                  tok(WA, 0), tok(WA, 0), full(aup_pad), full(gup_pad), full(vecs), full(wo16)],
        out_specs=tok(dm, 0),
        out_shape=jax.ShapeDtypeStruct((n, dm), jnp.float32),
        compiler_params=pltpu.CompilerParams(dimension_semantics=("arbitrary",), vmem_limit_bytes=VMEM_LIMIT),
        name="post_mix",
    )(x, mod6, y0, y1, proj, proj, proj, proj, proj, h0, h1, aup_pad, gup_pad, vecs, wo16)


def _shifted(u, shift, seq_len):
    n = u.shape[0]
    pos = lax.broadcasted_iota(jnp.int32, u.shape, 0) & (seq_len - 1)
    rolled = pltpu.roll(u, (-shift) % n, 0)
    ok = (pos + shift >= 0) & (pos + shift < seq_len)
    return jnp.where(ok, rolled, 0.0)


GELU_C0 = 0.7978845608028654
GELU_C1 = GELU_C0 * 0.044715


def _gelu_tanh(x):
    return (0.5 * x) * (1.0 + jnp.tanh(x * (GELU_C0 + GELU_C1 * (x * x))))


def _ffn_kernel(x_ref, mod_ref, g_ref, wu_ref, cw_ref, cb_ref, wd_ref, gf_ref, o_ref, hid_scr, *, seq_len, final_norm):
    x = x_ref[...]
    h16 = (_rms(x) * g_ref[...] * (1.0 + mod_ref[0, 4:5, :]) + mod_ref[0, 3:4, :]).astype(jnp.bfloat16)

    def conv(lo):
        cols = slice(lo, lo + FFN_CHUNK)
        u = jnp.dot(h16, wu_ref[:, cols], preferred_element_type=jnp.float32)
        return (cb_ref[:, cols] + _shifted(u, -1, seq_len) * cw_ref[0:1, cols] + u * cw_ref[1:2, cols]
                + _shifted(u, 1, seq_len) * cw_ref[2:3, cols])

    for j in range(D_FF // FFN_CHUNK):
        lo = j * FFN_CHUNK
        hid_scr[:, lo:lo + FFN_CHUNK] = (_gelu_tanh(conv(lo)) * conv(D_FF + lo)).astype(jnp.bfloat16)
    out = x + mod_ref[0, 5:6, :] * jnp.dot(hid_scr[...], wd_ref[...], preferred_element_type=jnp.float32)
    if final_norm:
        out = _rms(out) * gf_ref[...]
    o_ref[...] = out


def ffn(x, mod6, ln_g, wup16, conv_w, conv_b, wdown16, lnf_g, tiles_per_mod, seq_len, final_norm):
    n, dm = x.shape
    tm = FFN_TILE
    assert D_FF % FFN_CHUNK == 0 and tm % seq_len == 0 and n % tm == 0
    resident = lambda a: pl.BlockSpec(a.shape, lambda i: (0,) * a.ndim, pipeline_mode=pl.Buffered(1))
    cb = conv_b.reshape(1, 2 * D_FF)
    return pl.pallas_call(
        functools.partial(_ffn_kernel, seq_len=seq_len, final_norm=final_norm),
        grid=(n // tm,),
        in_specs=[pl.BlockSpec((tm, dm), lambda i: (i, 0)),
                  pl.BlockSpec((1, 6, dm), lambda i: (i // tiles_per_mod, 0, 0)),
                  pl.BlockSpec((1, dm), lambda i: (0, 0)),
                  resident(wup16), resident(conv_w), resident(cb), resident(wdown16),
                  pl.BlockSpec((1, dm), lambda i: (0, 0))],
        out_specs=pl.BlockSpec((tm, dm), lambda i: (i, 0)),
        out_shape=jax.ShapeDtypeStruct((n, dm), jnp.float32),
        scratch_shapes=[pltpu.VMEM((tm, D_FF), jnp.bfloat16)],
        compiler_params=pltpu.CompilerParams(dimension_semantics=("arbitrary",), vmem_limit_bytes=VMEM_LIMIT),
        name="ffn",
    )(x, mod6, ln_g.reshape(1, dm), wup16, conv_w, cb, wdown16, lnf_g.reshape(1, dm))


def _lru_scan_block(a, u, carry, reverse):
    n = a.shape[0]
    row = lax.broadcasted_iota(jnp.int32, a.shape, 0)
    s = 1
    while s < n:
        if s < SUBLANES:
            ok = (row < n - s) if reverse else (row >= s)
            shift = (n - s) if reverse else s
            a_prev = jnp.where(ok, pltpu.roll(a, shift, 0), 1.0)
            u_prev = jnp.where(ok, pltpu.roll(u, shift, 0), 0.0)
            u = a * u_prev + u
            a = a * a_prev
        elif reverse:
            u = jnp.concatenate([a[:n - s] * u[s:] + u[:n - s], u[n - s:]], axis=0)
            a = jnp.concatenate([a[:n - s] * a[s:], a[n - s:]], axis=0)
        else:
            u = jnp.concatenate([u[:s], a[s:] * u[:n - s] + u[s:]], axis=0)
            a = jnp.concatenate([a[:s], a[s:] * a[:n - s]], axis=0)
        s *= 2
    return a * carry + u


def _lru_kernel(*refs, seq_len, emit_state):
    xb_refs = refs[0:2]
    gw_ref, cw_ref, vec_ref, h0_ref = refs[2:6]
    h_refs = refs[6:8]
    pos = 8
    if emit_state:
        hfin_ref = refs[pos]
        pos += 1
    carry = refs[pos]
    c = pl.program_id(1)

    @pl.when(c == 0)
    def _():
        carry[...] = h0_ref[0]

    for d in range(2):
        xb = xb_refs[d][0]
        xc = vec_ref[0:1, :]
        for j in range(4):
            xc = xc + _shifted(xb, j - CONV_B_LEFT, seq_len) * cw_ref[j:j + 1, :]
        gates = jnp.dot(xc.astype(jnp.bfloat16), gw_ref[d], preferred_element_type=jnp.float32)
        rg = jax.nn.sigmoid(gates[:, :WB] + vec_ref[1 + d:2 + d, :])
        ig = jax.nn.sigmoid(gates[:, WB:] + vec_ref[3 + d:4 + d, :])
        log_a = -LRU_C * rg * vec_ref[5 + d:6 + d, :]
        a = jnp.exp(log_a)
        u = jnp.sqrt(-jnp.tanh(log_a) * (a * a + 1.0)) * (ig * xc)
        h = _lru_scan_block(a, u, carry[d:d + 1, :], reverse=(d == 1))
        h_refs[d][0] = h
        n = h.shape[0]
        carry[d:d + 1, :] = h[0:1, :] if d == 1 else h[n - 1:n, :]
    if emit_state:
        @pl.when(c == pl.num_programs(1) - 1)
        def _():
            hfin_ref[0] = carry[...]


def rglru(proj, gate_w16, conv_w, vecs, h0, seq_len, emit_state):
    bsz, t, _ = proj.shape
    tb = LRU_TB
    nt = t // tb
    assert t % tb == 0 and tb % seq_len == 0
    blk = lambda d: pl.BlockSpec((1, tb, WB), functools.partial(
        lambda bi, c, d, j: (bi, c if d == 0 else nt - 1 - c, j), d=d, j=COL_XB))
    oblk = lambda d: pl.BlockSpec((1, tb, WB), functools.partial(
        lambda bi, c, d: (bi, c if d == 0 else nt - 1 - c, 0), d=d))
    full = lambda a: pl.BlockSpec(a.shape, lambda bi, c: (0,) * a.ndim)
    sspec = pl.BlockSpec((1, 2, WB), lambda bi, c: (bi, 0, 0))
    out_specs = [oblk(0), oblk(1)]
    out_shape = [jax.ShapeDtypeStruct((bsz, t, WB), jnp.float32)] * 2
    if emit_state:
        out_specs.append(sspec)
        out_shape.append(jax.ShapeDtypeStruct((bsz, 2, WB), jnp.float32))
    return pl.pallas_call(
        functools.partial(_lru_kernel, seq_len=seq_len, emit_state=emit_state),
        grid=(bsz, nt),
        in_specs=[blk(0), blk(1), full(gate_w16), full(conv_w), full(vecs), sspec],
        out_specs=out_specs,
        out_shape=out_shape,
        scratch_shapes=[pltpu.VMEM((2, WB), jnp.float32)],
        compiler_params=pltpu.CompilerParams(
            dimension_semantics=("arbitrary", "arbitrary"), vmem_limit_bytes=VMEM_LIMIT),
        name="rglru",
    )(proj, proj, gate_w16, conv_w, vecs, h0)


def _block_diag(w):
    eye = jnp.eye(HB, dtype=w.dtype)
    return (w[:, :, None, :] * eye[:, None, :, None]).reshape(WB, WB)


def block(x, mod6, s0_bd, h0, emit_state, seq_len, final_norm, lp):
    bsz, t, dm = x.shape
    n = bsz * t
    n_mod = mod6.shape[0]
    assert (n // n_mod) % TOK_TILE == 0 and TOK_TILE % FFN_TILE == 0
    per_mod = lambda tile: (n // n_mod) // tile
    xf = x.reshape(n, dm)
    proj = in_proj(xf, mod6, lp['ln1_g'], lp['w_in16'], per_mod(TOK_TILE))
    proj3 = proj.reshape(bsz, t, -1)
    wkv = wkv7(proj3, lp['rw_wup_pad'], lp['rw_aup_pad'], lp['rw_vecs'], s0_bd, emit_state)
    lru = rglru(proj3, lp['lru_gate_w16'], lp['lru_conv_w'], lp['lru_vecs'], h0, min(seq_len, LRU_TB), emit_state)
    flat = lambda a: a.reshape(n, -1)
    xf = post_mix(xf, mod6, flat(wkv[0]), flat(wkv[1]), proj, flat(lru[0]), flat(lru[1]), lp['rw_aup_pad'],
                  lp['rw_gup_pad'], lp['post_vecs'], lp['w_out16'], per_mod(TOK_TILE))
    xf = ffn(xf, mod6, lp['ln2_g'], lp['w_up16'], lp['ffn_conv_w'], lp['ffn_conv_b'], lp['w_down16'], lp['lnf_g'],
             per_mod(FFN_TILE), seq_len, final_norm)
    return xf.reshape(bsz, t, dm), (wkv[2] if emit_state else None), (lru[2] if emit_state else None)


def kernel(x_prompt, x_sample, state_rwkv, state_lru, c, c_ctx, ln1_g, w_mod, b_mod, w_in, rw_w0, rw_w_up, rw_a0, rw_a_up, rw_g_up, rw_k_k, rw_k_a, rw_r_k, rw_lnx_g, rw_lnx_b, lru_conv_w, lru_conv_b, lru_ga_w, lru_ga_b, lru_gx_w, lru_gx_b, lru_lam, w_out, ln2_g, w_up, ffn_conv_w, ffn_conv_b, w_down, lnf_g):
    bf16 = jnp.bfloat16
    f32 = jnp.float32
    bp, tp, _ = x_prompt.shape
    bs = x_sample.shape[0]
    yp, ys = x_prompt, x_sample
    cc = jnp.concatenate([c, c_ctx[None], jnp.zeros((8 - bs - 1, D_MODEL), c.dtype)], axis=0)
    cc = jax.nn.silu(cc)
    mod_all = mod_proj(cc, w_mod) + b_mod[:, None, :]
    new_rwkv, new_lru = [], []
    zero_h = jnp.zeros((bp, 2, WB), f32)
    zeros2 = jnp.zeros((2, WA), f32)
    perm = jnp.concatenate([jnp.arange(0, SPLITS[2]), jnp.arange(SPLITS[5], SPLITS[6] + WB),
                            jnp.arange(SPLITS[2], SPLITS[5])])
    for l in range(DEPTH):
        gate_w = jnp.stack([jnp.concatenate([_block_diag(lru_ga_w[l, d]), _block_diag(lru_gx_w[l, d])], axis=1)
                            for d in range(2)])
        lp = {'ln1_g': ln1_g[l], 'w_in16': w_in[l][:, perm].astype(bf16),
              'rw_wup_pad': jnp.pad(rw_w_up[l], ((0, 0), (0, GW - R_W), (0, 0))).astype(bf16),
              'rw_aup_pad': jnp.pad(rw_a_up[l], ((0, 0), (R_W, GW - R_W - R_A), (0, 0))).astype(bf16),
              'rw_gup_pad': jnp.pad(rw_g_up[l], ((R_W + R_A, 0), (0, 0))).astype(bf16),
              'rw_vecs': jnp.concatenate([rw_w0[l], rw_a0[l], rw_k_k[l][None], rw_k_a[l][None], zeros2], axis=0),
              'post_vecs': jnp.concatenate([rw_a0[l], rw_k_a[l][None], rw_r_k[l].reshape(1, WA), rw_lnx_g[l][None],
                                            rw_lnx_b[l][None], zeros2], axis=0),
              'lru_gate_w16': gate_w.astype(bf16), 'lru_conv_w': lru_conv_w[l],
              'lru_vecs': jnp.concatenate([lru_conv_b[l][None], lru_ga_b[l], lru_gx_b[l],
                                           jax.nn.softplus(-lru_lam[l]), jnp.zeros((1, WB), f32)], axis=0),
              'w_out16': w_out[l].astype(bf16), 'ln2_g': ln2_g[l], 'w_up16': w_up[l].astype(bf16),
              'ffn_conv_w': ffn_conv_w[l], 'ffn_conv_b': ffn_conv_b[l], 'w_down16': w_down[l].astype(bf16),
              'lnf_g': lnf_g}
        mod = mod_all[l]
        final = l == DEPTH - 1
        yp, s_fin, h_fin = block(yp, mod[bs].reshape(1, 6, D_MODEL), None, zero_h, True, tp, final, lp)
        new_rwkv.append(s_fin)
        new_lru.append(h_fin)
        s0 = _to_bd(jnp.moveaxis(state_rwkv[:, l], 1, 0))
        ys, _, _ = block(ys, mod[:bs].reshape(bs, 6, D_MODEL), s0, state_lru[:, l], False, GRID_W, final, lp)
    return (yp, ys, jnp.stack(new_rwkv, axis=1), jnp.stack(new_lru, axis=1))
```
